```python
import jax, jax.numpy as jnp
from jax import lax
import numpy as np

D_MODEL = 2048
BATCH = 4
SEQ = 4096
DEPTH = 4

PLE_DIM = 256
D_FF = 4 * D_MODEL
EPS = 1e-6
NEG = -1e30
FORCE_BONUS = 1e6

LRU_WIDTH = D_MODEL // 2
LRU_HEADS = 16
LRU_HEAD_DIM = LRU_WIDTH // LRU_HEADS
CONV_WIDTH = 4
LRU_C = 8.0
POOL_WIDTH = D_MODEL // 2
POOL_WINDOWS = (2, 4, 8, 16)
POOL_GROUPS = len(POOL_WINDOWS)
POOL_GROUP_DIM = POOL_WIDTH // POOL_GROUPS
AB_IN = 2 * LRU_WIDTH + POOL_WIDTH
AB_MIX = LRU_WIDTH + POOL_WIDTH

N_HEADS = 16
HEAD_DIM = D_MODEL // N_HEADS
N_KV = 4
HPG = N_HEADS // N_KV
KV_WIDTH = N_KV * HEAD_DIM
CMP_LEN = 32
CMP_STRIDE = 16
CMP_HIDDEN = 512
SEL_LEN = 64
SEL_TOPN = 16
WINDOW = 512
N_BRANCH = 3
Q_BLOCK = 32
C_MIX = N_HEADS * HEAD_DIM
C_IN = C_MIX + 6 * KV_WIDTH + N_BRANCH * N_HEADS

N_EVEN = (DEPTH + 1) // 2
N_ODD = DEPTH // 2

kernel_name = 'hybrid_rglru_pool_nsa'


def rmsnorm(x, g):
    xf = x.astype(jnp.float32)
    y = xf * lax.rsqrt(jnp.mean(xf * xf, axis=-1, keepdims=True) + EPS)
    return (y * g.astype(jnp.float32)).astype(x.dtype)


def alibi_slopes():
    h = jnp.arange(1, N_HEADS + 1, dtype=jnp.float32)
    return jnp.exp2(-8.0 * h / N_HEADS).reshape(N_KV, HPG)


def causal_conv(x, w, b):
    S = x.shape[1]
    xp = jnp.pad(x, ((0, 0), (CONV_WIDTH - 1, 0), (0, 0)))
    out = b
    for k in range(CONV_WIDTH):
        out = out + xp[:, k:k + S] * w[k]
    return out


def _lin_combine(c1, c2):
    a1, b1 = c1
    a2, b2 = c2
    return a1 * a2, a2 * b1 + b2


def rg_lru(x, w_r, b_r, w_i, b_i, lam):
    B, S, W = x.shape
    xf = x.astype(jnp.float32)
    xh = xf.reshape(B, S, LRU_HEADS, LRU_HEAD_DIM)
    r = jax.nn.sigmoid(jnp.einsum('bshi,hij->bshj', xh, w_r.astype(jnp.float32)).reshape(B, S, W) + b_r.astype(jnp.float32))
    i = jax.nn.sigmoid(jnp.einsum('bshi,hij->bshj', xh, w_i.astype(jnp.float32)).reshape(B, S, W) + b_i.astype(jnp.float32))
    log_a = -LRU_C * r * jax.nn.softplus(-lam.astype(jnp.float32))
    a = jnp.exp(log_a)
    bvals = jnp.sqrt(-jnp.expm1(2.0 * log_a)) * (i * xf)
    _, h = lax.associative_scan(_lin_combine, (a, bvals), axis=1)
    return h.astype(x.dtype)


def pool_mixer(u, w, scale):
    B, S, _ = u.shape
    ug = u.astype(jnp.float32).reshape(B, S, POOL_GROUPS, POOL_GROUP_DIM)
    cs = jnp.cumsum(ug, axis=1)
    t = jnp.arange(S)
    outs = []
    for g, win in enumerate(POOL_WINDOWS):
        c = cs[:, :, g]
        prev = jnp.pad(c, ((0, 0), (win, 0), (0, 0)))[:, :S]
        cnt = jnp.minimum(t + 1, win).astype(jnp.float32)[None, :, None]
        outs.append((c - prev) / cnt - ug[:, :, g])
    d = jnp.stack(outs, axis=2)
    y = jnp.einsum('bsgi,gij->bsgj', d, w.astype(jnp.float32)).reshape(B, S, POOL_WIDTH)
    return (y * scale.astype(jnp.float32)).astype(u.dtype)


def ab_mixer(hn, w_in, conv_w, conv_b, w_r, b_r, w_i, b_i, lam, pool_w, pool_scale, w_out):
    z = hn @ w_in
    xr, gate, u = jnp.split(z, [LRU_WIDTH, 2 * LRU_WIDTH], axis=-1)
    xr = causal_conv(xr, conv_w, conv_b)
    y_lru = rg_lru(xr, w_r, b_r, w_i, b_i, lam) * jax.nn.gelu(gate)
    y_pool = pool_mixer(u, pool_w, pool_scale)
    return jnp.concatenate([y_lru, y_pool], axis=-1) @ w_out


def compress(kv, pos, w1, w2):
    S = kv.shape[1]
    n_cmp = (S - CMP_LEN) // CMP_STRIDE + 1
    idx = jnp.arange(n_cmp)[:, None] * CMP_STRIDE + jnp.arange(CMP_LEN)[None, :]
    blocks = kv[:, idx] + pos[None, None, :, None, :]
    hid = jax.nn.gelu(jnp.einsum('bnlgd,ldf->bngf', blocks, w1))
    return jnp.einsum('bngf,fd->bngd', hid, w2)


def nsa_mixer(hn, w_in, pos_k, w1_k, w2_k, pos_v, w1_v, w2_v, w_out):
    B, S, _ = hn.shape
    z = hn @ w_in
    splits = np.cumsum([C_MIX] + [KV_WIDTH] * 6).tolist()
    q, kc, vc, ks, vs, kw, vw, gl = jnp.split(z, splits, axis=-1)
    q = q.reshape(B, S, N_KV, HPG, HEAD_DIM) * (HEAD_DIM ** -0.5)
    kv_shape = (B, S, N_KV, HEAD_DIM)
    kc, vc, ks, vs, kw, vw = [a.reshape(kv_shape) for a in (kc, vc, ks, vs, kw, vw)]
    gates = jax.nn.sigmoid(gl.astype(jnp.float32)).reshape(B, S, N_BRANCH, N_KV, HPG)

    k_cmp = compress(kc, pos_k, w1_k, w2_k).astype(jnp.float32)
    v_cmp = compress(vc, pos_v, w1_v, w2_v).astype(jnp.float32)
    n_cmp = k_cmp.shape[1]
    n_sel = S // SEL_LEN
    top_n = min(SEL_TOPN, n_sel)
    cmp_start = jnp.arange(n_cmp) * CMP_STRIDE
    cmp_end = cmp_start + CMP_LEN - 1
    sel_start = jnp.arange(n_sel) * SEL_LEN
    ov = ((cmp_start[:, None] < sel_start[None, :] + SEL_LEN) & (cmp_start[:, None] + CMP_LEN > sel_start[None, :])).astype(jnp.float32)
    ks_blocks = ks.reshape(B, n_sel, SEL_LEN, N_KV, HEAD_DIM).transpose(0, 3, 1, 2, 4)
    vs_blocks = vs.reshape(B, n_sel, SEL_LEN, N_KV, HEAD_DIM).transpose(0, 3, 1, 2, 4)
    kw_pad = jnp.pad(kw, ((0, 0), (WINDOW, 0), (0, 0), (0, 0)))
    vw_pad = jnp.pad(vw, ((0, 0), (WINDOW, 0), (0, 0), (0, 0)))
    slopes = alibi_slopes()
    n_qblk = S // Q_BLOCK
    q_blocks = q.reshape(B, n_qblk, Q_BLOCK, N_KV, HPG, HEAD_DIM).swapaxes(0, 1)
    g_blocks = gates.reshape(B, n_qblk, Q_BLOCK, N_BRANCH, N_KV, HPG).swapaxes(0, 1)
    bi = jnp.arange(B)[:, None, None, None]
    gi = jnp.arange(N_KV)[None, :, None, None]
    jj = jnp.arange(n_sel)

    def one_block(args):
        blk, qb, gb = args
        t0 = blk * Q_BLOCK
        t = t0 + jnp.arange(Q_BLOCK)
        qf = qb.astype(jnp.float32)
        dist_c = (t[:, None] - cmp_end[None, :]).astype(jnp.float32)
        valid_c = dist_c >= 0
        s_c = jnp.einsum('bqghd,bngd->bghqn', qf, k_cmp) - slopes[None, :, :, None, None] * dist_c
        s_c = jnp.where(valid_c, s_c, NEG)
        p_c = jnp.where(valid_c, jax.nn.softmax(s_c, axis=-1), 0.0)
        o_c = jnp.einsum('bghqn,bngd->bqghd', p_c, v_cmp)
        imp = jnp.einsum('bghqn,nm->bgqm', p_c, ov)
        cur = t // SEL_LEN
        forced = (jj[None, :] == 0) | (jj[None, :] == cur[:, None]) | (jj[None, :] == cur[:, None] - 1)
        causal_s = sel_start[None, :] <= t[:, None]
        imp = jnp.where(forced, imp + FORCE_BONUS, imp)
        imp = jnp.where(causal_s, imp, NEG)
        _, idx = lax.top_k(imp, top_n)
        k_g = ks_blocks[bi, gi, idx].astype(jnp.float32)
        v_g = vs_blocks[bi, gi, idx].astype(jnp.float32)
        pos = idx[..., None] * SEL_LEN + jnp.arange(SEL_LEN)
        dist_s = (t[None, None, :, None, None] - pos).astype(jnp.float32)[:, :, None]
        s_s = jnp.einsum('bqghd,bgqnkd->bghqnk', qf, k_g) - slopes[None, :, :, None, None, None] * dist_s
        s_s = jnp.where(dist_s >= 0, s_s, NEG).reshape(B, N_KV, HPG, Q_BLOCK, top_n * SEL_LEN)
        p_s = jax.nn.softmax(s_s, axis=-1).reshape(B, N_KV, HPG, Q_BLOCK, top_n, SEL_LEN)
        o_s = jnp.einsum('bghqnk,bgqnkd->bqghd', p_s, v_g)
        kwb = lax.dynamic_slice_in_dim(kw_pad, t0, WINDOW + Q_BLOCK, axis=1).astype(jnp.float32)
        vwb = lax.dynamic_slice_in_dim(vw_pad, t0, WINDOW + Q_BLOCK, axis=1).astype(jnp.float32)
        s_pos = t0 - WINDOW + jnp.arange(WINDOW + Q_BLOCK)
        dist_w = t[:, None] - s_pos[None, :]
        valid_w = (dist_w >= 0) & (dist_w < WINDOW) & (s_pos[None, :] >= 0)
        s_w = jnp.einsum('bqghd,bkgd->bghqk', qf, kwb) - slopes[None, :, :, None, None] * dist_w.astype(jnp.float32)
        p_w = jax.nn.softmax(jnp.where(valid_w, s_w, NEG), axis=-1)
        o_w = jnp.einsum('bghqk,bkgd->bqghd', p_w, vwb)
        o = gb[:, :, 0, :, :, None] * o_c + gb[:, :, 1, :, :, None] * o_s + gb[:, :, 2, :, :, None] * o_w
        return o.astype(hn.dtype)

    o = lax.map(one_block, (jnp.arange(n_qblk), q_blocks, g_blocks))
    o = o.swapaxes(0, 1).reshape(B, S, C_MIX)
    return o @ w_out


def setup_inputs(seed: int = 0) -> dict:
    key = jax.random.key(seed)
    keys = iter(jax.random.split(key, 48))

    def nrm(shape, scale):
        return jax.random.normal(next(keys), shape, jnp.float32) * scale

    def gain(shape):
        return 1.0 + nrm(shape, 0.05)

    u = jax.random.uniform(next(keys), (N_EVEN, LRU_WIDTH), jnp.float32, 0.9, 0.999)
    s = u ** (1.0 / LRU_C)
    lam = jnp.log(s) - jnp.log1p(-s)
    return {
        'x': nrm((BATCH, SEQ, D_MODEL), 1.0),
        'p': nrm((DEPTH, BATCH, SEQ, PLE_DIM), 1.0),
        'ln_mix_g': gain((DEPTH, D_MODEL)),
        'ab_w_in': nrm((N_EVEN, D_MODEL, AB_IN), D_MODEL ** -0.5),
        'ab_conv_w': nrm((N_EVEN, CONV_WIDTH, LRU_WIDTH), CONV_WIDTH ** -0.5),
        'ab_conv_b': nrm((N_EVEN, LRU_WIDTH), 0.01),
        'ab_w_rgate': nrm((N_EVEN, LRU_HEADS, LRU_HEAD_DIM, LRU_HEAD_DIM), LRU_HEAD_DIM ** -0.5),
        'ab_b_rgate': nrm((N_EVEN, LRU_WIDTH), 0.1),
        'ab_w_igate': nrm((N_EVEN, LRU_HEADS, LRU_HEAD_DIM, LRU_HEAD_DIM), LRU_HEAD_DIM ** -0.5),
        'ab_b_igate': nrm((N_EVEN, LRU_WIDTH), 0.1),
        'ab_lambda': lam,
        'ab_pool_w': nrm((N_EVEN, POOL_GROUPS, POOL_GROUP_DIM, POOL_GROUP_DIM), POOL_GROUP_DIM ** -0.5),
        'ab_pool_scale': gain((N_EVEN, POOL_WIDTH)),
        'ab_w_out': nrm((N_EVEN, AB_MIX, D_MODEL), AB_MIX ** -0.5),
        'c_w_in': nrm((N_ODD, D_MODEL, C_IN), D_MODEL ** -0.5),
        'c_cmp_pos_k': nrm((N_ODD, CMP_LEN, HEAD_DIM), 0.02),
        'c_cmp_w1_k': nrm((N_ODD, CMP_LEN, HEAD_DIM, CMP_HIDDEN), (CMP_LEN * HEAD_DIM) ** -0.5),
        'c_cmp_w2_k': nrm((N_ODD, CMP_HIDDEN, HEAD_DIM), CMP_HIDDEN ** -0.5),
        'c_cmp_pos_v': nrm((N_ODD, CMP_LEN, HEAD_DIM), 0.02),
        'c_cmp_w1_v': nrm((N_ODD, CMP_LEN, HEAD_DIM, CMP_HIDDEN), (CMP_LEN * HEAD_DIM) ** -0.5),
        'c_cmp_w2_v': nrm((N_ODD, CMP_HIDDEN, HEAD_DIM), CMP_HIDDEN ** -0.5),
        'c_w_out': nrm((N_ODD, C_MIX, D_MODEL), C_MIX ** -0.5),
        'ln_mlp_g': gain((DEPTH, D_MODEL)),
        'mlp_w_up': nrm((DEPTH, D_MODEL, D_FF), D_MODEL ** -0.5),
        'mlp_w_down': nrm((DEPTH, D_FF, D_MODEL), D_FF ** -0.5),
        'ln_ple_g': gain((DEPTH, D_MODEL)),
        'ple_w_gate': nrm((DEPTH, D_MODEL, D_MODEL), D_MODEL ** -0.5),
        'ple_w_proj': nrm((DEPTH, PLE_DIM, D_MODEL), PLE_DIM ** -0.5),
        'ln_final_g': gain((D_MODEL,)),
    }


def reference(x, p, ln_mix_g, ab_w_in, ab_conv_w, ab_conv_b, ab_w_rgate, ab_b_rgate, ab_w_igate, ab_b_igate, ab_lambda, ab_pool_w, ab_pool_scale, ab_w_out, c_w_in, c_cmp_pos_k, c_cmp_w1_k, c_cmp_w2_k, c_cmp_pos_v, c_cmp_w1_v, c_cmp_w2_v, c_w_out, ln_mlp_g, mlp_w_up, mlp_w_down, ln_ple_g, ple_w_gate, ple_w_proj, ln_final_g):
    h = x
    for i in range(DEPTH):
        j = i // 2
        hn = rmsnorm(h, ln_mix_g[i])
        if i % 2 == 0:
            h = h + ab_mixer(hn, ab_w_in[j], ab_conv_w[j], ab_conv_b[j], ab_w_rgate[j], ab_b_rgate[j], ab_w_igate[j], ab_b_igate[j], ab_lambda[j], ab_pool_w[j], ab_pool_scale[j], ab_w_out[j])
        else:
            h = h + nsa_mixer(hn, c_w_in[j], c_cmp_pos_k[j], c_cmp_w1_k[j], c_cmp_w2_k[j], c_cmp_pos_v[j], c_cmp_w1_v[j], c_cmp_w2_v[j], c_w_out[j])
        hn = rmsnorm(h, ln_mlp_g[i])
        h = h + jnp.square(jax.nn.relu(hn @ mlp_w_up[i])) @ mlp_w_down[i]
        gate = jax.nn.sigmoid(rmsnorm(h, ln_ple_g[i]) @ ple_w_gate[i])
        h = h + gate * (p[i] @ ple_w_proj[i])
    return rmsnorm(h, ln_final_g)
```

```python
import functools

import jax
import jax.numpy as jnp
import numpy as np
from jax import lax
from jax.experimental import pallas as pl
from jax.experimental.pallas import tpu as pltpu

F32 = jnp.float32
BF16 = jnp.bfloat16

EPS = 1e-6
NEG = -1e30
FORCE_BONUS = 1e6
LRU_C = 8.0

LANES = 128
V7X_SCOPED_VMEM_CAP = 60000 * 1024
COMPILER_TEMP_BYTES = 12 * 1024 * 1024

LRU_HEADS = 16
POOL_WINDOWS = (2, 4, 8, 16)
CONV_WIDTH = 4
N_HEADS = 16
N_KV = 4
HPG = N_HEADS // N_KV
HEAD_DIM = 128
CMP_LEN = 32
CMP_STRIDE = 16
SEL_LEN = 64
SEL_TOPN = 16
WINDOW = 512
N_BRANCH = 3

ROW_TILE = 1024
COL_TILE = 1024
MLP_ROW_TILE = 512
MLP_FF_TILE = 1024
MIX_TIME_TILE = 256
ATT_Q_TILE = 128
ATT_K_TILE = 512


def _params(n_axes, *block_bytes):
    need = 2 * sum(block_bytes) + COMPILER_TEMP_BYTES
    return pltpu.CompilerParams(
        dimension_semantics=("arbitrary",) * n_axes,
        vmem_limit_bytes=int(min(need, V7X_SCOPED_VMEM_CAP)),
    )


def _rms_bf16(x, g):
    ms = jnp.mean(x * x, axis=-1, keepdims=True)
    return (x * lax.rsqrt(ms + EPS) * g).astype(BF16)


def _sigmoid(x):
    return 1.0 / (1.0 + jnp.exp(-x))


def _gelu_tanh(x):
    c = np.float32(np.sqrt(2.0 / np.pi))
    return x * (0.5 * (1.0 + jnp.tanh(c * (x + 0.044715 * (x * x * x)))))


def _dot(a, b):
    return jnp.dot(a, b, preferred_element_type=F32)


def _dot_nt(a, b):
    return lax.dot_general(a, b, (((1,), (1,)), ((), ())), preferred_element_type=F32)


def _norm_matmul_kernel(x_ref, g_ref, w_ref, o_ref, xn_ref):
    @pl.when(pl.program_id(1) == 0)
    def _():
        xn_ref[...] = _rms_bf16(x_ref[...], g_ref[...])

    o_ref[...] = _dot(xn_ref[...], w_ref[...]).astype(o_ref.dtype)


def _norm_matmul(x, g, w, out_dtype, tn):
    m, k = x.shape
    n = w.shape[1]
    tm = min(ROW_TILE, m)
    osz = jnp.dtype(out_dtype).itemsize
    return pl.pallas_call(
        _norm_matmul_kernel,
        grid=(m // tm, n // tn),
        in_specs=[
            pl.BlockSpec((tm, k), lambda i, j: (i, 0)),
            pl.BlockSpec((1, k), lambda i, j: (0, 0)),
            pl.BlockSpec((k, tn), lambda i, j: (0, j)),
        ],
        out_specs=pl.BlockSpec((tm, tn), lambda i, j: (i, j)),
        out_shape=jax.ShapeDtypeStruct((m, n), out_dtype),
        scratch_shapes=[pltpu.VMEM((tm, k), BF16)],
        compiler_params=_params(2, tm * k * 4, k * tn * 2, tm * tn * osz, tm * k),
        name="norm_matmul",
    )(x, g.reshape(1, k), w)


def _matmul_residual_kernel(y_ref, w_ref, h_ref, o_ref):
    o_ref[...] = h_ref[...] + _dot(y_ref[...], w_ref[...])


def _matmul_residual(y, w, h):
    m, k = y.shape
    n = w.shape[1]
    tm = min(ROW_TILE, m)
    tn = min(COL_TILE, n)
    return pl.pallas_call(
        _matmul_residual_kernel,
        grid=(m // tm, n // tn),
        in_specs=[
            pl.BlockSpec((tm, k), lambda i, j: (i, 0)),
            pl.BlockSpec((k, tn), lambda i, j: (0, j)),
            pl.BlockSpec((tm, tn), lambda i, j: (i, j)),
        ],
        out_specs=pl.BlockSpec((tm, tn), lambda i, j: (i, j)),
        out_shape=jax.ShapeDtypeStruct((m, n), F32),
        compiler_params=_params(2, tm * k * 2, k * tn * 2, tm * tn * 4, tm * tn * 4),
        name="matmul_residual",
    )(y, w, h)


def _mlp_kernel(x_ref, g_ref, wu_ref, wd_ref, o_ref, xn_ref, acc_ref):
    j = pl.program_id(1)

    @pl.when(j == 0)
    def _():
        xn_ref[...] = _rms_bf16(x_ref[...], g_ref[...])
        acc_ref[...] = jnp.zeros_like(acc_ref)

    u = _dot(xn_ref[...], wu_ref[...])
    u = jnp.square(jnp.maximum(u, 0.0)).astype(BF16)
    acc_ref[...] += _dot(u, wd_ref[...])

    @pl.when(j == pl.num_programs(1) - 1)
    def _():
        o_ref[...] = x_ref[...] + acc_ref[...]


def _mlp(x, g, w_up, w_down):
    m, d = x.shape
    f = w_up.shape[1]
    tm = min(MLP_ROW_TILE, m)
    tf = min(MLP_FF_TILE, f)
    return pl.pallas_call(
        _mlp_kernel,
        grid=(m // tm, f // tf),
        in_specs=[
            pl.BlockSpec((tm, d), lambda i, j: (i, 0)),
            pl.BlockSpec((1, d), lambda i, j: (0, 0)),
            pl.BlockSpec((d, tf), lambda i, j: (0, j)),
            pl.BlockSpec((tf, d), lambda i, j: (j, 0)),
        ],
        out_specs=pl.BlockSpec((tm, d), lambda i, j: (i, 0)),
        out_shape=jax.ShapeDtypeStruct((m, d), F32),
        scratch_shapes=[pltpu.VMEM((tm, d), BF16), pltpu.VMEM((tm, d), F32)],
        compiler_params=_params(2, tm * d * 4, d * tf * 2, tf * d * 2, tm * d * 4, tm * d * 3),
        name="mlp",
    )(x, g.reshape(1, d), w_up, w_down)


def _ple_kernel(x_ref, hcol_ref, g_ref, wg_ref, p_ref, wp_ref, o_ref, xn_ref):
    @pl.when(pl.program_id(1) == 0)
    def _():
        xn_ref[...] = _rms_bf16(x_ref[...], g_ref[...])

    gate = _sigmoid(_dot(xn_ref[...], wg_ref[...]))
    proj = _dot(p_ref[...].astype(BF16), wp_ref[...])
    o_ref[...] = hcol_ref[...] + gate * proj


def _ple(x, g, w_gate, p, w_proj):
    m, d = x.shape
    e = p.shape[1]
    tm = min(ROW_TILE, m)
    tn = min(COL_TILE, d)
    return pl.pallas_call(
        _ple_kernel,
        grid=(m // tm, d // tn),
        in_specs=[
            pl.BlockSpec((tm, d), lambda i, j: (i, 0)),
            pl.BlockSpec((tm, tn), lambda i, j: (i, j)),
            pl.BlockSpec((1, d), lambda i, j: (0, 0)),
            pl.BlockSpec((d, tn), lambda i, j: (0, j)),
            pl.BlockSpec((tm, e), lambda i, j: (i, 0)),
            pl.BlockSpec((e, tn), lambda i, j: (0, j)),
        ],
        out_specs=pl.BlockSpec((tm, tn), lambda i, j: (i, j)),
        out_shape=jax.ShapeDtypeStruct((m, d), F32),
        scratch_shapes=[pltpu.VMEM((tm, d), BF16)],
        compiler_params=_params(
            2, tm * d * 4, tm * tn * 4, d * tn * 2, tm * e * 4, e * tn * 2, tm * tn * 4, tm * d
        ),
        name="ple",
    )(x, x, g.reshape(1, d), w_gate, p, w_proj)


def _final_norm_kernel(x_ref, g_ref, o_ref):
    x = x_ref[...]
    ms = jnp.mean(x * x, axis=-1, keepdims=True)
    o_ref[...] = x * lax.rsqrt(ms + EPS) * g_ref[...]


def _final_norm(x, g):
    m, d = x.shape
    tm = min(ROW_TILE, m)
    return pl.pallas_call(
        _final_norm_kernel,
        grid=(m // tm,),
        in_specs=[pl.BlockSpec((tm, d), lambda i: (i, 0)), pl.BlockSpec((1, d), lambda i: (0, 0))],
        out_specs=pl.BlockSpec((tm, d), lambda i: (i, 0)),
        out_shape=jax.ShapeDtypeStruct((m, d), F32),
        compiler_params=_params(1, tm * d * 4, tm * d * 4),
        name="final_norm",
    )(x, g.reshape(1, d))


def _mixer_kernel(
    z_ref, cw_ref, cb_ref, wr_ref, br_ref, wi_ref, bi_ref, lam_ref, pw_ref, ps_ref,
    y_ref, xext_ref, uext_ref, hlast_ref, *, ts, width,
):
    s = pl.program_id(1)
    xh = 8
    uh = 16

    @pl.when(s == 0)
    def _():
        xext_ref[0:xh, :] = jnp.zeros((xh, width), F32)
        uext_ref[0:uh, :] = jnp.zeros((uh, width), F32)
        hlast_ref[...] = jnp.zeros_like(hlast_ref)

    xr = z_ref[:, 0:width]
    gate = z_ref[:, width:2 * width]
    u = z_ref[:, 2 * width:3 * width]

    xext_ref[xh:xh + ts, :] = xr
    conv = cb_ref[...] + cw_ref[CONV_WIDTH - 1:CONV_WIDTH, :] * xr
    for k in range(CONV_WIDTH - 1):
        shift = CONV_WIDTH - 1 - k
        conv = conv + cw_ref[k:k + 1, :] * xext_ref[pl.ds(xh - shift, ts), :]
    xext_ref[0:xh, :] = xr[ts - xh:ts, :]

    gw = wr_ref.shape[1]
    conv_b = conv.astype(BF16)
    r_parts, i_parts = [], []
    for c in range(width // gw):
        xc = conv_b[:, c * gw:(c + 1) * gw]
        r_parts.append(_dot(xc, wr_ref[c]))
        i_parts.append(_dot(xc, wi_ref[c]))
    r = _sigmoid(jnp.concatenate(r_parts, axis=1) + br_ref[...])
    ig = _sigmoid(jnp.concatenate(i_parts, axis=1) + bi_ref[...])

    nl = -lam_ref[...]
    softplus = jnp.maximum(nl, 0.0) + jnp.log(1.0 + jnp.exp(-jnp.abs(nl)))
    log_a = -LRU_C * r * softplus
    a = jnp.exp(log_a)
    b = jnp.sqrt(1.0 - jnp.exp(2.0 * log_a)) * (ig * conv)

    row = lax.broadcasted_iota(jnp.int32, (ts, 1), 0)
    d = 1
    while d < ts:
        keep = row >= d
        a_sh = jnp.where(keep, pltpu.roll(a, d, axis=0), 1.0)
        b_sh = jnp.where(keep, pltpu.roll(b, d, axis=0), 0.0)
        b = a * b_sh + b
        a = a * a_sh
        d *= 2
    h = a * hlast_ref[...] + b
    hlast_ref[...] = h[ts - 1:ts, :]
    y_ref[:, 0:width] = (h * _gelu_tanh(gate)).astype(BF16)

    uext_ref[uh:uh + ts, :] = u
    t = s * ts + row
    gd = width // len(POOL_WINDOWS)
    for gi, win in enumerate(POOL_WINDOWS):
        cs = slice(gi * gd, (gi + 1) * gd)
        wsum = u[:, cs]
        for k in range(1, win):
            wsum = wsum + uext_ref[pl.ds(uh - k, ts), cs]
        cnt = jnp.minimum(t + 1, win).astype(F32)
        dd = wsum / cnt - u[:, cs]
        yp = _dot(dd.astype(BF16), pw_ref[gi]) * ps_ref[:, cs]
        y_ref[:, width + gi * gd:width + (gi + 1) * gd] = yp.astype(BF16)
    uext_ref[0:uh, :] = u[ts - uh:ts, :]


def _mixer(z, conv_w, conv_b, w_r, b_r, w_i, b_i, lam, pool_w, pool_scale):
    bsz, seq, w3 = z.shape
    width = w3 // 3
    ts = min(MIX_TIME_TILE, seq)
    hd = width // LRU_HEADS
    per = 2 * LANES // hd
    gw = per * hd

    def block_diag(w):
        w = w.reshape(LRU_HEADS // per, per, hd, hd)
        eye = jnp.eye(per, dtype=w.dtype)
        return jnp.einsum("cpij,pq->cpiqj", w, eye).reshape(LRU_HEADS // per, gw, gw).astype(BF16)

    row = lambda v: v.reshape(1, width)
    const2 = lambda shape: pl.BlockSpec(shape, lambda b, s: (0,) * len(shape))
    kern = functools.partial(_mixer_kernel, ts=ts, width=width)
    gd = width // len(POOL_WINDOWS)
    return pl.pallas_call(
        kern,
        grid=(bsz, seq // ts),
        in_specs=[
            pl.BlockSpec((None, ts, w3), lambda b, s: (b, s, 0)),
            const2((CONV_WIDTH, width)),
            const2((1, width)),
            const2((LRU_HEADS // per, gw, gw)),
            const2((1, width)),
            const2((LRU_HEADS // per, gw, gw)),
            const2((1, width)),
            const2((1, width)),
            const2((len(POOL_WINDOWS), gd, gd)),
            const2((1, width)),
        ],
        out_specs=pl.BlockSpec((None, ts, 2 * width), lambda b, s: (b, s, 0)),
        out_shape=jax.ShapeDtypeStruct((bsz, seq, 2 * width), BF16),
        scratch_shapes=[
            pltpu.VMEM((ts + 8, width), F32),
            pltpu.VMEM((ts + 16, width), F32),
            pltpu.VMEM((1, width), F32),
        ],
        compiler_params=_params(2, ts * w3 * 4, ts * width * 4, ts * width * 16),
        name="lru_pool_mixer",
    )(
        z, conv_w, row(conv_b), block_diag(w_r), row(b_r), block_diag(w_i), row(b_i), row(lam),
        pool_w.astype(BF16), row(pool_scale),
    )


def _compress_kernel(x_ref, pos_ref, w1_ref, w2_ref, o_ref):
    x = x_ref[...].astype(F32)
    xa = (x + pos_ref[0]).astype(BF16)
    xb = (x + pos_ref[1]).astype(BF16)
    first = _dot(xa, w1_ref[0])
    second = _dot(xb, w1_ref[1])
    n = x.shape[0]
    hid = first + pltpu.roll(second, n - 1, axis=0)
    o_ref[...] = _dot(_gelu_tanh(hid).astype(BF16), w2_ref[...]).astype(o_ref.dtype)


def _compress(xflat, pos, w1, w2):
    two, bsz, ng, nc, kk = xflat.shape
    hid = w1.shape[-1]
    dh = w2.shape[-1]
    return pl.pallas_call(
        _compress_kernel,
        grid=(two, bsz, ng),
        in_specs=[
            pl.BlockSpec((None, None, None, nc, kk), lambda t, b, g: (t, b, g, 0, 0)),
            pl.BlockSpec((None, 2, 1, kk), lambda t, b, g: (t, 0, 0, 0)),
            pl.BlockSpec((None, 2, kk, hid), lambda t, b, g: (t, 0, 0, 0)),
            pl.BlockSpec((None, hid, dh), lambda t, b, g: (t, 0, 0)),
        ],
        out_specs=pl.BlockSpec((None, None, None, nc, dh), lambda t, b, g: (t, b, g, 0, 0)),
        out_shape=jax.ShapeDtypeStruct((two, bsz, ng, nc, dh), BF16),
        compiler_params=_params(3, nc * kk * 2, 2 * kk * hid * 2, hid * dh * 2, nc * dh * 2, nc * kk * 8),
        name="nsa_compress",
    )(xflat, pos, w1, w2)


def _nsa_kernel(
    slope_ref, q_ref, kc_ref, vc_ref, ks_ref, vs_ref, kw_ref, vw_ref, gt_ref, ovt_ref, et_ref,
    o_ref, kp_ref, *, tq, tk, seq,
):
    g = pl.program_id(1)
    i = pl.program_id(2)
    t0 = i * tq
    rows = HPG * tq
    n_sel = seq // SEL_LEN
    ncp = kc_ref.shape[0]

    @pl.when(i == 0)
    def _():
        kp_ref[:, 0:HEAD_DIM] = ks_ref[...]
        kp_ref[:, HEAD_DIM:2 * HEAD_DIM] = et_ref[...]

    q = q_ref[...]
    qs = jnp.concatenate([q[:, h * HEAD_DIM:(h + 1) * HEAD_DIM] for h in range(HPG)], axis=0)
    qs = (qs.astype(F32) * np.float32(HEAD_DIM ** -0.5)).astype(BF16)

    ridx = lax.broadcasted_iota(jnp.int32, (rows, 1), 0)
    slope = jnp.zeros((rows, 1), F32)
    for h in range(HPG):
        in_h = (ridx >= h * tq) & (ridx < (h + 1) * tq)
        slope = jnp.where(in_h, slope_ref[g * HPG + h], slope)
    t = t0 + jnp.concatenate([lax.broadcasted_iota(jnp.int32, (tq, 1), 0)] * HPG, axis=0)

    cend = lax.broadcasted_iota(jnp.int32, (1, ncp), 1) * CMP_STRIDE + (CMP_LEN - 1)
    s_c = _dot_nt(qs, kc_ref[...]) + slope * cend.astype(F32)
    valid_c = cend <= t
    s_c = jnp.where(valid_c, s_c, NEG)
    m_c = jnp.max(s_c, axis=1, keepdims=True)
    p_c = jnp.where(valid_c, jnp.exp(s_c - m_c), 0.0)
    l_c = jnp.sum(p_c, axis=1, keepdims=True)
    p_c = p_c / jnp.where(l_c > 0.0, l_c, 1.0)
    o_c = _dot(p_c.astype(BF16), vc_ref[...])

    p_sum = p_c[0:tq]
    for h in range(1, HPG):
        p_sum = p_sum + p_c[h * tq:(h + 1) * tq]
    p_hi = p_sum.astype(BF16)
    rem = p_sum - p_hi.astype(F32)
    p_mid = rem.astype(BF16)
    p_lo = (rem - p_mid.astype(F32)).astype(BF16)
    ovt = ovt_ref[...]
    imp = _dot_nt(ovt, p_hi) + _dot_nt(ovt, p_mid) + _dot_nt(ovt, p_lo)

    blk = lax.broadcasted_iota(jnp.int32, (n_sel, tq), 0)
    tl = t0 + lax.broadcasted_iota(jnp.int32, (n_sel, tq), 1)
    cur = lax.shift_right_arithmetic(tl, int(np.log2(SEL_LEN)))
    forced = (blk == 0) | (blk == cur) | (blk == cur - 1)
    causal_s = blk * SEL_LEN <= tl
    v = jnp.where(forced, imp + FORCE_BONUS, imp)
    v = jnp.where(causal_s, v, NEG)
    sub = 8
    rank_slabs = []
    for r0 in range(0, n_sel, sub):
        vs_ = v[r0:r0 + sub, :]
        lower = blk[r0:r0 + sub, :]
        rk = jnp.zeros((sub, tq), F32)
        for j in range(n_sel):
            vj = v[j:j + 1, :]
            if j < r0:
                beats = vj >= vs_
            elif j >= r0 + sub - 1:
                beats = vj > vs_
            else:
                beats = (vj > vs_) | ((vj == vs_) & (lower > j))
            rk = rk + jnp.where(beats, 1.0, 0.0)
        rank_slabs.append(rk)
    rank = jnp.concatenate(rank_slabs, axis=0)
    top_n = min(SEL_TOPN, n_sel)
    sel_t = jnp.where((rank < top_n) & causal_s, 1.0, 0.0)
    sel_t = jnp.concatenate([sel_t, jnp.ones((LANES - n_sel, tq), F32)], axis=0)
    sel = sel_t.T
    mask_bias = ((sel - 1.0) * 1e30).astype(BF16)
    qa = jnp.concatenate([qs, jnp.concatenate([mask_bias] * HPG, axis=0)], axis=1)

    kcol = lax.broadcasted_iota(jnp.int32, (1, tk), 1)

    def sel_tile(kt, carry, causal):
        m_p, l_p, acc = carry
        j0 = pl.multiple_of(kt * tk, tk)
        kpos = j0 + kcol
        s = _dot_nt(qa, kp_ref[pl.ds(j0, tk), :]) + slope * kpos.astype(F32)
        if causal:
            s = jnp.where(kpos <= t, s, NEG)
        m_n = jnp.maximum(m_p, jnp.max(s, axis=1, keepdims=True))
        p = jnp.exp(s - m_n)
        alpha = jnp.exp(m_p - m_n)
        l_n = alpha * l_p + jnp.sum(p, axis=1, keepdims=True)
        acc = alpha * acc + _dot(p.astype(BF16), vs_ref[pl.ds(j0, tk), :])
        return m_n, l_n, acc

    n_kt = (t0 + tq - 1) // tk + 1
    init = (jnp.full((rows, 1), NEG, F32), jnp.zeros((rows, 1), F32), jnp.zeros((rows, HEAD_DIM), F32))
    carry = lax.fori_loop(0, n_kt - 1, lambda kt, c: sel_tile(kt, c, False), init)
    _, l_s, acc_s = sel_tile(n_kt - 1, carry, True)
    o_s = acc_s / l_s

    span = WINDOW + tq
    start = pl.multiple_of(jnp.maximum(t0 - WINDOW, 0), tq)
    wpos = start + lax.broadcasted_iota(jnp.int32, (1, span), 1)
    dist = t - wpos
    valid_w = (dist >= 0) & (dist < WINDOW)
    s_w = _dot_nt(qs, kw_ref[pl.ds(start, span), :]) + slope * wpos.astype(F32)
    s_w = jnp.where(valid_w, s_w, NEG)
    m_w = jnp.max(s_w, axis=1, keepdims=True)
    p_w = jnp.exp(s_w - m_w)
    l_w = jnp.sum(p_w, axis=1, keepdims=True)
    o_w = _dot(p_w.astype(BF16), vw_ref[pl.ds(start, span), :]) / l_w

    gates = _sigmoid(gt_ref[...])
    for h in range(HPG):
        rs = slice(h * tq, (h + 1) * tq)
        o_h = (
            gates[:, h:h + 1] * o_c[rs]
            + gates[:, HPG + h:HPG + h + 1] * o_s[rs]
            + gates[:, 2 * HPG + h:2 * HPG + h + 1] * o_w[rs]
        )
        o_ref[:, h * HEAD_DIM:(h + 1) * HEAD_DIM] = o_h.astype(o_ref.dtype)


def _nsa_attention(z, kv_cmp, gates):
    bsz, seq, _ = z.shape
    tq = min(ATT_Q_TILE, seq)
    tk = min(ATT_K_TILE, seq)
    n_sel = seq // SEL_LEN
    ncp = kv_cmp.shape[3]
    qw = HPG * HEAD_DIM
    kv0 = N_HEADS

    cs = np.arange(ncp)[None, :] * CMP_STRIDE
    ss = np.arange(n_sel)[:, None] * SEL_LEN
    ovt = ((cs < ss + SEL_LEN) & (cs + CMP_LEN > ss) & (np.arange(ncp)[None, :] < ncp - 1)).astype(np.float32)
    member = (np.arange(seq)[:, None] // SEL_LEN == np.arange(LANES)[None, :]).astype(np.float32)
    slopes = jnp.exp2(-8.0 * jnp.arange(1, N_HEADS + 1, dtype=F32) / N_HEADS)

    def kv_spec(col):
        return pl.BlockSpec((None, seq, HEAD_DIM), lambda b, g, i, col=col: (b, 0, kv0 + col * N_KV + g))

    kern = functools.partial(_nsa_kernel, tq=tq, tk=tk, seq=seq)
    return pl.pallas_call(
        kern,
        grid=(bsz, N_KV, seq // tq),
        in_specs=[
            pl.BlockSpec(memory_space=pltpu.SMEM),
            pl.BlockSpec((None, tq, qw), lambda b, g, i: (b, i, g)),
            pl.BlockSpec((None, None, None, ncp, HEAD_DIM), lambda b, g, i: (0, b, g, 0, 0)),
            pl.BlockSpec((None, None, None, ncp, HEAD_DIM), lambda b, g, i: (1, b, g, 0, 0)),
            kv_spec(2), kv_spec(3), kv_spec(4), kv_spec(5),
            pl.BlockSpec((None, None, tq, LANES), lambda b, g, i: (b, g, i, 0)),
            pl.BlockSpec((n_sel, ncp), lambda b, g, i: (0, 0)),
            pl.BlockSpec((seq, LANES), lambda b, g, i: (0, 0)),
        ],
        out_specs=pl.BlockSpec((None, tq, qw), lambda b, g, i: (b, i, g)),
        out_shape=jax.ShapeDtypeStruct((bsz, seq, N_HEADS * HEAD_DIM), BF16),
        scratch_shapes=[pltpu.VMEM((seq, 2 * HEAD_DIM), BF16)],
        compiler_params=_params(3, 5 * seq * HEAD_DIM * 2, seq * HEAD_DIM * 2, HPG * tq * tk * 16),
        name="nsa_attention",
    )(slopes, z, kv_cmp, kv_cmp, z, z, z, z, gates, jnp.asarray(ovt, BF16), jnp.asarray(member, BF16))


def _ab_layer(h, bsz, seq, ln_g, w_in, conv_w, conv_b, w_r, b_r, w_i, b_i, lam, pool_w, pool_scale, w_out):
    d = h.shape[1]
    z = _norm_matmul(h, ln_g, w_in.astype(BF16), F32, COL_TILE)
    y = _mixer(z.reshape(bsz, seq, -1), conv_w, conv_b, w_r, b_r, w_i, b_i, lam, pool_w, pool_scale)
    return _matmul_residual(y.reshape(bsz * seq, -1), w_out.astype(BF16), h)


def _nsa_layer(h, bsz, seq, ln_g, w_in, pos_k, w1_k, w2_k, pos_v, w1_v, w2_v, w_out):
    d = h.shape[1]
    n_main = N_HEADS * HEAD_DIM + 6 * N_KV * HEAD_DIM
    n_gate = N_BRANCH * N_HEADS
    z = _norm_matmul(h, ln_g, w_in[:, :n_main].astype(BF16), BF16, COL_TILE)
    w_gate = jnp.pad(w_in[:, n_main:], ((0, 0), (0, LANES - n_gate))).astype(BF16)
    gl = _norm_matmul(h, ln_g, w_gate, F32, LANES)
    z = z.reshape(bsz, seq, n_main)

    nc = seq // CMP_STRIDE
    c0 = N_HEADS * HEAD_DIM
    zc = z[:, :, c0:c0 + 2 * N_KV * HEAD_DIM].reshape(bsz, nc, CMP_STRIDE, 2, N_KV, HEAD_DIM)
    xflat = zc.transpose(3, 0, 4, 1, 2, 5).reshape(2, bsz, N_KV, nc, CMP_STRIDE * HEAD_DIM)
    half = CMP_LEN // CMP_STRIDE
    pos = jnp.stack([pos_k, pos_v]).reshape(2, half, 1, CMP_STRIDE * HEAD_DIM)
    w1 = jnp.stack([w1_k, w1_v]).reshape(2, half, CMP_STRIDE * HEAD_DIM, -1).astype(BF16)
    w2 = jnp.stack([w2_k, w2_v]).astype(BF16)
    kv_cmp = _compress(xflat, pos, w1, w2)

    gates = gl[:, :n_gate].reshape(bsz, seq, N_BRANCH, N_KV, HPG).transpose(0, 3, 1, 2, 4)
    gates = jnp.pad(gates.reshape(bsz, N_KV, seq, N_BRANCH * HPG), ((0, 0),) * 3 + ((0, LANES - N_BRANCH * HPG),))

    o = _nsa_attention(z, kv_cmp, gates)
    return _matmul_residual(o.reshape(bsz * seq, -1), w_out.astype(BF16), h)


def kernel(x, p, ln_mix_g, ab_w_in, ab_conv_w, ab_conv_b, ab_w_rgate, ab_b_rgate, ab_w_igate, ab_b_igate, ab_lambda, ab_pool_w, ab_pool_scale, ab_w_out, c_w_in, c_cmp_pos_k, c_cmp_w1_k, c_cmp_w2_k, c_cmp_pos_v, c_cmp_w1_v, c_cmp_w2_v, c_w_out, ln_mlp_g, mlp_w_up, mlp_w_down, ln_ple_g, ple_w_gate, ple_w_proj, ln_final_g):
    bsz, seq, d = x.shape
    depth = p.shape[0]
    h = x.reshape(bsz * seq, d)
    for i in range(depth):
        j = i // 2
        if i % 2 == 0:
            h = _ab_layer(
                h, bsz, seq, ln_mix_g[i], ab_w_in[j], ab_conv_w[j], ab_conv_b[j], ab_w_rgate[j], ab_b_rgate[j],
                ab_w_igate[j], ab_b_igate[j], ab_lambda[j], ab_pool_w[j], ab_pool_scale[j], ab_w_out[j],
            )
        else:
            h = _nsa_layer(
                h, bsz, seq, ln_mix_g[i], c_w_in[j], c_cmp_pos_k[j], c_cmp_w1_k[j], c_cmp_w2_k[j],
                c_cmp_pos_v[j], c_cmp_w1_v[j], c_cmp_w2_v[j], c_w_out[j],
            )
        h = _mlp(h, ln_mlp_g[i], mlp_w_up[i].astype(BF16), mlp_w_down[i].astype(BF16))
        h = _ple(h, ln_ple_g[i], ple_w_gate[i].astype(BF16), p[i].reshape(bsz * seq, -1), ple_w_proj[i].astype(BF16))
    return _final_norm(h, ln_final_g).reshape(bsz, seq, d)
```

```python
import functools

import jax
import jax.numpy as jnp
import numpy as np
from jax import lax
from jax.experimental import pallas as pl
from jax.experimental.pallas import tpu as pltpu

F32 = jnp.float32
BF16 = jnp.bfloat16

EPS = 1e-6
NEG = -1e30
FORCE_BONUS = 1e6
LRU_C = 8.0

LANES = 128
V7X_SCOPED_VMEM_CAP = 60000 * 1024
COMPILER_TEMP_BYTES = 12 * 1024 * 1024

LRU_HEADS = 16
POOL_WINDOWS = (2, 4, 8, 16)
CONV_WIDTH = 4
N_HEADS = 16
N_KV = 4
HPG = N_HEADS // N_KV
HEAD_DIM = 128
CMP_LEN = 32
CMP_STRIDE = 16
SEL_LEN = 64
SEL_TOPN = 16
WINDOW = 512
N_BRANCH = 3

ROW_TILE = 1024
COL_TILE = 1024
MLP_ROW_TILE = 512
MLP_FF_TILE = 1024
PLE_ROW_TILE = 512
MIX_TIME_TILE = 256
ATT_Q_TILE = 128
ATT_K_TILE = 512


def _params(n_axes, *block_bytes):
    need = 2 * sum(block_bytes) + COMPILER_TEMP_BYTES
    return pltpu.CompilerParams(
        dimension_semantics=("arbitrary",) * n_axes,
        vmem_limit_bytes=int(min(need, V7X_SCOPED_VMEM_CAP)),
    )


def _rms_bf16(x, g):
    ms = jnp.mean(x * x, axis=-1, keepdims=True)
    return (x * lax.rsqrt(ms + EPS) * g).astype(BF16)


def _sigmoid(x):
    return 1.0 / (1.0 + jnp.exp(-x))


def _gelu_tanh(x):
    c = np.float32(np.sqrt(2.0 / np.pi))
    return x * (0.5 * (1.0 + jnp.tanh(c * (x + 0.044715 * (x * x * x)))))


def _dot(a, b):
    return jnp.dot(a, b, preferred_element_type=F32)


def _dot_nt(a, b):
    return lax.dot_general(a, b, (((1,), (1,)), ((), ())), preferred_element_type=F32)


def _norm_matmul_kernel(x_ref, g_ref, w_ref, o_ref, xn_ref):
    @pl.when(pl.program_id(1) == 0)
    def _():
        xn_ref[...] = _rms_bf16(x_ref[...], g_ref[...])

    o_ref[...] = _dot(xn_ref[...], w_ref[...]).astype(o_ref.dtype)


def _norm_matmul(x, g, w, out_dtype, tn):
    m, k = x.shape
    n = w.shape[1]
    tm = min(ROW_TILE, m)
    osz = jnp.dtype(out_dtype).itemsize
    return pl.pallas_call(
        _norm_matmul_kernel,
        grid=(m // tm, n // tn),
        in_specs=[
            pl.BlockSpec((tm, k), lambda i, j: (i, 0)),
            pl.BlockSpec((1, k), lambda i, j: (0, 0)),
            pl.BlockSpec((k, tn), lambda i, j: (0, j)),
        ],
        out_specs=pl.BlockSpec((tm, tn), lambda i, j: (i, j)),
        out_shape=jax.ShapeDtypeStruct((m, n), out_dtype),
        scratch_shapes=[pltpu.VMEM((tm, k), BF16)],
        compiler_params=_params(2, tm * k * 4, k * tn * 2, tm * tn * osz, tm * k),
        name="norm_matmul",
    )(x, g.reshape(1, k), w)


def _matmul_residual_kernel(y_ref, w_ref, h_ref, o_ref):
    o_ref[...] = h_ref[...] + _dot(y_ref[...], w_ref[...])


def _matmul_residual(y, w, h):
    m, k = y.shape
    n = w.shape[1]
    tm = min(ROW_TILE, m)
    tn = min(COL_TILE, n)
    return pl.pallas_call(
        _matmul_residual_kernel,
        grid=(m // tm, n // tn),
        in_specs=[
            pl.BlockSpec((tm, k), lambda i, j: (i, 0)),
            pl.BlockSpec((k, tn), lambda i, j: (0, j)),
            pl.BlockSpec((tm, tn), lambda i, j: (i, j)),
        ],
        out_specs=pl.BlockSpec((tm, tn), lambda i, j: (i, j)),
        out_shape=jax.ShapeDtypeStruct((m, n), F32),
        compiler_params=_params(2, tm * k * 2, k * tn * 2, tm * tn * 4, tm * tn * 4),
        name="matmul_residual",
    )(y, w, h)


def _mlp_kernel(x_ref, g_ref, wu_ref, wd_ref, o_ref, xn_ref, acc_ref):
    j = pl.program_id(1)

    @pl.when(j == 0)
    def _():
        xn_ref[...] = _rms_bf16(x_ref[...], g_ref[...])
        acc_ref[...] = jnp.zeros_like(acc_ref)

    u = _dot(xn_ref[...], wu_ref[...])
    u = jnp.square(jnp.maximum(u, 0.0)).astype(BF16)
    acc_ref[...] += _dot(u, wd_ref[...])

    @pl.when(j == pl.num_programs(1) - 1)
    def _():
        o_ref[...] = x_ref[...] + acc_ref[...]


def _mlp(x, g, w_up, w_down):
    m, d = x.shape
    f = w_up.shape[1]
    tm = min(MLP_ROW_TILE, m)
    tf = min(MLP_FF_TILE, f)
    return pl.pallas_call(
        _mlp_kernel,
        grid=(m // tm, f // tf),
        in_specs=[
            pl.BlockSpec((tm, d), lambda i, j: (i, 0)),
            pl.BlockSpec((1, d), lambda i, j: (0, 0)),
            pl.BlockSpec((d, tf), lambda i, j: (0, j)),
            pl.BlockSpec((tf, d), lambda i, j: (j, 0)),
        ],
        out_specs=pl.BlockSpec((tm, d), lambda i, j: (i, 0)),
        out_shape=jax.ShapeDtypeStruct((m, d), F32),
        scratch_shapes=[pltpu.VMEM((tm, d), BF16), pltpu.VMEM((tm, d), F32)],
        compiler_params=_params(2, tm * d * 4, d * tf * 2, tf * d * 2, tm * d * 4, tm * d * 3),
        name="mlp",
    )(x, g.reshape(1, d), w_up, w_down)


def _ple_kernel(x_ref, g_ref, wg_ref, p_ref, wp_ref, gf_ref, o_ref, *, final_norm):
    x = x_ref[...]
    gate = _sigmoid(_dot(_rms_bf16(x, g_ref[...]), wg_ref[...]))
    proj = _dot(p_ref[...].astype(BF16), wp_ref[...])
    h = x + gate * proj
    if final_norm:
        ms = jnp.mean(h * h, axis=-1, keepdims=True)
        h = h * lax.rsqrt(ms + EPS) * gf_ref[...]
    o_ref[...] = h


def _ple(x, g, w_gate, p_all, layer, w_proj, g_final, final_norm):
    m, d = x.shape
    e = p_all.shape[-1]
    tm = min(PLE_ROW_TILE, m)
    const = lambda shape: pl.BlockSpec(shape, lambda i: (0,) * len(shape))
    return pl.pallas_call(
        functools.partial(_ple_kernel, final_norm=final_norm),
        grid=(m // tm,),
        in_specs=[
            pl.BlockSpec((tm, d), lambda i: (i, 0)),
            const((1, d)),
            const((d, d)),
            pl.BlockSpec((None, tm, e), lambda i: (layer, i, 0)),
            const((e, d)),
            const((1, d)),
        ],
        out_specs=pl.BlockSpec((tm, d), lambda i: (i, 0)),
        out_shape=jax.ShapeDtypeStruct((m, d), F32),
        compiler_params=_params(1, tm * d * 4, d * d * 2, tm * e * 4, e * d * 2, tm * d * 4, tm * d * 6),
        name="ple",
    )(x, g.reshape(1, d), w_gate, p_all, w_proj, g_final.reshape(1, d))


def _mixer_kernel(
    z_ref, cw_ref, cb_ref, wr_ref, br_ref, wi_ref, bi_ref, lam_ref, pw_ref, ps_ref,
    y_ref, xext_ref, uext_ref, hlast_ref, *, ts, width,
):
    s = pl.program_id(1)
    xh = 8
    uh = 16

    @pl.when(s == 0)
    def _():
        xext_ref[0:xh, :] = jnp.zeros((xh, width), F32)
        uext_ref[0:uh, :] = jnp.zeros((uh, width), F32)
        hlast_ref[...] = jnp.zeros_like(hlast_ref)

    xr = z_ref[:, 0:width]
    gate = z_ref[:, width:2 * width]
    u = z_ref[:, 2 * width:3 * width]

    xext_ref[xh:xh + ts, :] = xr
    conv = cb_ref[...] + cw_ref[CONV_WIDTH - 1:CONV_WIDTH, :] * xr
    for k in range(CONV_WIDTH - 1):
        shift = CONV_WIDTH - 1 - k
        conv = conv + cw_ref[k:k + 1, :] * xext_ref[pl.ds(xh - shift, ts), :]
    xext_ref[0:xh, :] = xr[ts - xh:ts, :]

    gw = wr_ref.shape[1]
    conv_b = conv.astype(BF16)
    r_parts, i_parts = [], []
    for c in range(width // gw):
        xc = conv_b[:, c * gw:(c + 1) * gw]
        r_parts.append(_dot(xc, wr_ref[c]))
        i_parts.append(_dot(xc, wi_ref[c]))
    r = _sigmoid(jnp.concatenate(r_parts, axis=1) + br_ref[...])
    ig = _sigmoid(jnp.concatenate(i_parts, axis=1) + bi_ref[...])

    nl = -lam_ref[...]
    softplus = jnp.maximum(nl, 0.0) + jnp.log(1.0 + jnp.exp(-jnp.abs(nl)))
    log_a = -LRU_C * r * softplus
    a = jnp.exp(log_a)
    b = jnp.sqrt(1.0 - a * a) * (ig * conv)

    sub = 8
    srow = lax.broadcasted_iota(jnp.int32, (sub, 1), 0)
    carry = hlast_ref[...]
    h_groups = []
    for r0 in range(0, ts, sub):
        ag = a[r0:r0 + sub, :]
        bg = b[r0:r0 + sub, :]
        d = 1
        while d < sub:
            keep = srow >= d
            a_sh = jnp.where(keep, pltpu.roll(ag, d, axis=0), 1.0)
            b_sh = jnp.where(keep, pltpu.roll(bg, d, axis=0), 0.0)
            bg = ag * b_sh + bg
            ag = ag * a_sh
            d *= 2
        hg = ag * carry + bg
        carry = hg[sub - 1:sub, :]
        h_groups.append(hg)
    h = jnp.concatenate(h_groups, axis=0)
    hlast_ref[...] = carry
    y_ref[:, 0:width] = (h * _gelu_tanh(gate)).astype(BF16)

    uext_ref[uh:uh + ts, :] = u
    t = s * ts + lax.broadcasted_iota(jnp.int32, (ts, 1), 0)
    gd = width // len(POOL_WINDOWS)
    for gi, win in enumerate(POOL_WINDOWS):
        cs = slice(gi * gd, (gi + 1) * gd)
        wsum = u[:, cs]
        for k in range(1, win):
            wsum = wsum + uext_ref[pl.ds(uh - k, ts), cs]
        cnt = jnp.minimum(t + 1, win).astype(F32)
        dd = wsum / cnt - u[:, cs]
        yp = _dot(dd.astype(BF16), pw_ref[gi]) * ps_ref[:, cs]
        y_ref[:, width + gi * gd:width + (gi + 1) * gd] = yp.astype(BF16)
    uext_ref[0:uh, :] = u[ts - uh:ts, :]


def _mixer(z, conv_w, conv_b, w_r, b_r, w_i, b_i, lam, pool_w, pool_scale):
    bsz, seq, w3 = z.shape
    width = w3 // 3
    ts = min(MIX_TIME_TILE, seq)
    hd = width // LRU_HEADS
    per = 2 * LANES // hd
    gw = per * hd

    def block_diag(w):
        w = w.reshape(LRU_HEADS // per, per, hd, hd)
        eye = jnp.eye(per, dtype=w.dtype)
        return jnp.einsum("cpij,pq->cpiqj", w, eye).reshape(LRU_HEADS // per, gw, gw).astype(BF16)

    row = lambda v: v.reshape(1, width)
    const2 = lambda shape: pl.BlockSpec(shape, lambda b, s: (0,) * len(shape))
    kern = functools.partial(_mixer_kernel, ts=ts, width=width)
    gd = width // len(POOL_WINDOWS)
    return pl.pallas_call(
        kern,
        grid=(bsz, seq // ts),
        in_specs=[
            pl.BlockSpec((None, ts, w3), lambda b, s: (b, s, 0)),
            const2((CONV_WIDTH, width)),
            const2((1, width)),
            const2((LRU_HEADS // per, gw, gw)),
            const2((1, width)),
            const2((LRU_HEADS // per, gw, gw)),
            const2((1, width)),
            const2((1, width)),
            const2((len(POOL_WINDOWS), gd, gd)),
            const2((1, width)),
        ],
        out_specs=pl.BlockSpec((None, ts, 2 * width), lambda b, s: (b, s, 0)),
        out_shape=jax.ShapeDtypeStruct((bsz, seq, 2 * width), BF16),
        scratch_shapes=[
            pltpu.VMEM((ts + 8, width), F32),
            pltpu.VMEM((ts + 16, width), F32),
            pltpu.VMEM((1, width), F32),
        ],
        compiler_params=_params(2, ts * w3 * 4, ts * width * 4, ts * width * 16),
        name="lru_pool_mixer",
    )(
        z, conv_w, row(conv_b), block_diag(w_r), row(b_r), block_diag(w_i), row(b_i), row(lam),
        pool_w.astype(BF16), row(pool_scale),
    )


def _compress_kernel(x_ref, pos_ref, w1_ref, w2_ref, o_ref):
    x = x_ref[...].astype(F32)
    xa = (x + pos_ref[0]).astype(BF16)
    xb = (x + pos_ref[1]).astype(BF16)
    first = _dot(xa, w1_ref[0])
    second = _dot(xb, w1_ref[1])
    n = x.shape[0]
    hid = first + pltpu.roll(second, n - 1, axis=0)
    o_ref[...] = _dot(_gelu_tanh(hid).astype(BF16), w2_ref[...]).astype(o_ref.dtype)


def _compress(xflat, pos, w1, w2):
    two, bsz, ng, nc, kk = xflat.shape
    hid = w1.shape[-1]
    dh = w2.shape[-1]
    return pl.pallas_call(
        _compress_kernel,
        grid=(two, bsz, ng),
        in_specs=[
            pl.BlockSpec((None, None, None, nc, kk), lambda t, b, g: (t, b, g, 0, 0)),
            pl.BlockSpec((None, 2, 1, kk), lambda t, b, g: (t, 0, 0, 0)),
            pl.BlockSpec((None, 2, kk, hid), lambda t, b, g: (t, 0, 0, 0)),
            pl.BlockSpec((None, hid, dh), lambda t, b, g: (t, 0, 0)),
        ],
        out_specs=pl.BlockSpec((None, None, None, nc, dh), lambda t, b, g: (t, b, g, 0, 0)),
        out_shape=jax.ShapeDtypeStruct((two, bsz, ng, nc, dh), BF16),
        compiler_params=_params(3, nc * kk * 2, 2 * kk * hid * 2, hid * dh * 2, nc * dh * 2, nc * kk * 8),
        name="nsa_compress",
    )(xflat, pos, w1, w2)


LOG2E = np.float32(1.4426950408889634)
AUG_MASK_ROWS = 64
ALIBI_PARTS = 3


def _t_bf16(x):
    return x.astype(F32).T.astype(BF16)


def _nsa_kernel(
    slope_ref, q_ref, kc_ref, vc_ref, ks_ref, vs_ref, kw_ref, vw_ref, gt_ref, ovt_ref, aug_ref, augc_ref,
    o_ref, kp_ref, kwp_ref, kcp_ref, vst_ref, vwt_ref, vct_ref, s_ref, *, tq, tk, seq,
):
    g = pl.program_id(1)
    i = pl.program_id(2)
    t0 = i * tq
    nl = HPG * tq
    n_sel = seq // SEL_LEN
    ncp = kc_ref.shape[0]

    @pl.when(i == 0)
    def _():
        kp_ref[:, 0:HEAD_DIM] = ks_ref[...]
        kp_ref[:, HEAD_DIM:2 * HEAD_DIM] = aug_ref[...]
        kwp_ref[:, 0:HEAD_DIM] = kw_ref[...]
        kwp_ref[:, HEAD_DIM:2 * HEAD_DIM] = aug_ref[...]
        kcp_ref[:, 0:HEAD_DIM] = kc_ref[...]
        kcp_ref[:, HEAD_DIM:2 * HEAD_DIM] = augc_ref[...]
        for kt in range(seq // tk):
            vst_ref[kt] = _t_bf16(vs_ref[kt * tk:(kt + 1) * tk, :])
        for kb in range(seq // tq):
            vwt_ref[kb] = _t_bf16(vw_ref[kb * tq:(kb + 1) * tq, :])
        vct_ref[...] = _t_bf16(vc_ref[...])

    q = q_ref[...]
    qscale = np.float32(HEAD_DIM ** -0.5) * LOG2E
    q_t = jnp.concatenate(
        [(q[:, h * HEAD_DIM:(h + 1) * HEAD_DIM].astype(F32) * qscale).T for h in range(HPG)], axis=1
    ).astype(BF16)

    lane = lax.broadcasted_iota(jnp.int32, (1, nl), 1)
    slope = jnp.zeros((1, nl), F32)
    for h in range(HPG):
        in_h = (lane >= h * tq) & (lane < (h + 1) * tq)
        slope = jnp.where(in_h, slope_ref[g * HPG + h] * LOG2E, slope)
    t_row = t0 + (lane & (tq - 1))

    arow = lax.broadcasted_iota(jnp.int32, (LANES - AUG_MASK_ROWS, nl), 0)
    alibi_rows = jnp.zeros((LANES - AUG_MASK_ROWS, nl), F32)
    rem = slope
    for k in range(ALIBI_PARTS):
        piece = rem.astype(BF16).astype(F32)
        rem = rem - piece
        alibi_rows = jnp.where(arow == 2 * k, piece * float(SEL_LEN), alibi_rows)
        alibi_rows = jnp.where(arow == 2 * k + 1, piece, alibi_rows)
    qa_plain = jnp.concatenate(
        [q_t, jnp.zeros((AUG_MASK_ROWS, nl), BF16), alibi_rows.astype(BF16)], axis=0
    )

    cend = lax.broadcasted_iota(jnp.int32, (ncp, 1), 0) * CMP_STRIDE + (CMP_LEN - 1)
    valid_c = cend <= t_row
    s_c = jnp.where(valid_c, _dot(kcp_ref[...], qa_plain), NEG)
    m_c = jnp.max(s_c, axis=0, keepdims=True)
    p_c = jnp.where(valid_c, jnp.exp2(s_c - m_c), 0.0)
    l_c = jnp.sum(p_c, axis=0, keepdims=True)
    p_c = p_c * (1.0 / jnp.where(l_c > 0.0, l_c, 1.0))
    o_c = _dot(vct_ref[...], p_c.astype(BF16))

    p_sum = p_c[:, 0:tq]
    for h in range(1, HPG):
        p_sum = p_sum + p_c[:, h * tq:(h + 1) * tq]
    ovt = ovt_ref[...]
    imp = jnp.zeros((n_sel, tq), F32)
    rem = p_sum
    for _ in range(3):
        piece = rem.astype(BF16)
        rem = rem - piece.astype(F32)
        imp = imp + _dot(ovt, piece)

    blk = lax.broadcasted_iota(jnp.int32, (n_sel, tq), 0)
    tl = t0 + lax.broadcasted_iota(jnp.int32, (n_sel, tq), 1)
    cur = lax.shift_right_arithmetic(tl, int(np.log2(SEL_LEN)))
    forced = (blk == 0) | (blk == cur) | (blk == cur - 1)
    causal_s = blk * SEL_LEN <= tl
    v = jnp.where(forced, imp + FORCE_BONUS, imp)
    v = jnp.where(causal_s, v, NEG)
    sub = 8
    rank_slabs = []
    for r0 in range(0, n_sel, sub):
        vs_ = v[r0:r0 + sub, :]
        lower = blk[r0:r0 + sub, :]
        rk = jnp.zeros((sub, tq), F32)
        for j in range(n_sel):
            vj = v[j:j + 1, :]
            if j < r0:
                beats = vj >= vs_
            elif j >= r0 + sub - 1:
                beats = vj > vs_
            else:
                beats = (vj > vs_) | ((vj == vs_) & (lower > j))
            rk = rk + jnp.where(beats, 1.0, 0.0)
        rank_slabs.append(rk)
    rank = jnp.concatenate(rank_slabs, axis=0)
    top_n = min(SEL_TOPN, n_sel)
    mask_rows = jnp.where((rank < top_n) & causal_s, 0.0, NEG)
    if n_sel < AUG_MASK_ROWS:
        mask_rows = jnp.concatenate([mask_rows, jnp.zeros((AUG_MASK_ROWS - n_sel, tq), F32)], axis=0)
    mask_rows = jnp.concatenate([mask_rows] * HPG, axis=1).astype(BF16)
    qa_sel = jnp.concatenate([q_t, mask_rows, alibi_rows.astype(BF16)], axis=0)

    span = WINDOW + tq
    start = pl.multiple_of(jnp.maximum(t0 - WINDOW, 0), tq)
    dist = t_row - (start + lax.broadcasted_iota(jnp.int32, (span, 1), 0))
    valid_w = lax.bitcast_convert_type(dist, jnp.uint32) < jnp.uint32(WINDOW)
    s_w = jnp.where(valid_w, _dot(kwp_ref[pl.ds(start, span), :], qa_plain), NEG)
    m_w = jnp.max(s_w, axis=0, keepdims=True)
    p_w = jnp.exp2(s_w - m_w)
    l_w = jnp.sum(p_w, axis=0, keepdims=True)
    p_wb = p_w.astype(BF16)
    sb = start // tq
    acc_w = jnp.zeros((HEAD_DIM, nl), F32)
    for k in range(span // tq):
        acc_w = acc_w + _dot(vwt_ref[sb + k], p_wb[k * tq:(k + 1) * tq, :])

    krow = lax.broadcasted_iota(jnp.int32, (tk, 1), 0)

    def score_tile(kt, masked):
        j0 = pl.multiple_of(kt * tk, tk)
        s = _dot(kp_ref[pl.ds(j0, tk), :], qa_sel)
        if masked:
            s = jnp.where(j0 + krow <= t_row, s, NEG)
        s_ref[...] = s
        return jnp.max(s, axis=0, keepdims=True)

    def finish_tile(kt, m_p, l_p, acc, m_t):
        m_n = jnp.maximum(m_p, m_t)
        p = jnp.exp2(s_ref[...] - m_n)
        alpha = jnp.exp2(m_p - m_n)
        l_n = alpha * l_p + jnp.sum(p, axis=0, keepdims=True)
        acc = alpha * acc + _dot(vst_ref[kt], p.astype(BF16))
        return m_n, l_n, acc

    n_kt = (t0 + tq - 1) // tk + 1

    def pipe_step(kt, carry):
        m_p, l_p, acc, m_t, prev = carry
        m_n, l_n, acc = finish_tile(prev, m_p, l_p, acc, m_t)
        return m_n, l_n, acc, score_tile(kt, False), kt

    carry = (
        jnp.full((1, nl), NEG, F32), jnp.zeros((1, nl), F32), jnp.zeros((HEAD_DIM, nl), F32),
        score_tile(n_kt - 1, True), n_kt - 1,
    )
    m_p, l_p, acc, m_t, prev = lax.fori_loop(0, n_kt - 1, pipe_step, carry)
    _, l_s, acc_s = finish_tile(prev, m_p, l_p, acc, m_t)

    gates = _sigmoid(gt_ref[...])

    def gate_row(br):
        return jnp.concatenate([gates[br * HPG + h:br * HPG + h + 1, :] for h in range(HPG)], axis=1)

    o_t = (
        gate_row(0) * o_c
        + (gate_row(1) * (1.0 / l_s)) * acc_s
        + (gate_row(2) * (1.0 / l_w)) * acc_w
    )
    for h in range(HPG):
        o_ref[:, h * HEAD_DIM:(h + 1) * HEAD_DIM] = o_t[:, h * tq:(h + 1) * tq].T.astype(o_ref.dtype)


def _nsa_attention(z, kv_cmp, gates):
    bsz, seq, _ = z.shape
    tq = min(ATT_Q_TILE, seq)
    tk = min(ATT_K_TILE, seq)
    n_sel = seq // SEL_LEN
    ncp = kv_cmp.shape[3]
    qw = HPG * HEAD_DIM
    kv0 = N_HEADS
    assert n_sel <= AUG_MASK_ROWS and tq & (tq - 1) == 0 and seq % tk == 0 and tk % tq == 0

    cs = np.arange(ncp)[None, :] * CMP_STRIDE
    ss = np.arange(n_sel)[:, None] * SEL_LEN
    ovt = ((cs < ss + SEL_LEN) & (cs + CMP_LEN > ss) & (np.arange(ncp)[None, :] < ncp - 1)).astype(np.float32)

    def aug_cols(pos):
        a = np.zeros((pos.shape[0], LANES), np.float32)
        a[np.arange(pos.shape[0]), np.minimum(pos // SEL_LEN, AUG_MASK_ROWS - 1)] = 1.0
        for k in range(ALIBI_PARTS):
            a[:, AUG_MASK_ROWS + 2 * k] = pos // SEL_LEN
            a[:, AUG_MASK_ROWS + 2 * k + 1] = pos % SEL_LEN
        return jnp.asarray(a, BF16)

    aug = aug_cols(np.arange(seq))
    augc = aug_cols(np.arange(ncp) * CMP_STRIDE + CMP_LEN - 1)
    slopes = jnp.exp2(-8.0 * jnp.arange(1, N_HEADS + 1, dtype=F32) / N_HEADS)

    def kv_spec(col):
        return pl.BlockSpec((None, seq, HEAD_DIM), lambda b, g, i, col=col: (b, 0, kv0 + col * N_KV + g))

    const = lambda shape: pl.BlockSpec(shape, lambda b, g, i: (0,) * len(shape))
    kern = functools.partial(_nsa_kernel, tq=tq, tk=tk, seq=seq)
    return pl.pallas_call(
        kern,
        grid=(bsz, N_KV, seq // tq),
        in_specs=[
            pl.BlockSpec(memory_space=pltpu.SMEM),
            pl.BlockSpec((None, tq, qw), lambda b, g, i: (b, i, g)),
            pl.BlockSpec((None, None, None, ncp, HEAD_DIM), lambda b, g, i: (0, b, g, 0, 0)),
            pl.BlockSpec((None, None, None, ncp, HEAD_DIM), lambda b, g, i: (1, b, g, 0, 0)),
            kv_spec(2), kv_spec(3), kv_spec(4), kv_spec(5),
            pl.BlockSpec((None, None, 16, tq), lambda b, g, i: (b, g, 0, i)),
            const((n_sel, ncp)),
            const((seq, LANES)),
            const((ncp, LANES)),
        ],
        out_specs=pl.BlockSpec((None, tq, qw), lambda b, g, i: (b, i, g)),
        out_shape=jax.ShapeDtypeStruct((bsz, seq, N_HEADS * HEAD_DIM), BF16),
        scratch_shapes=[
            pltpu.VMEM((seq, 2 * HEAD_DIM), BF16),
            pltpu.VMEM((seq, 2 * HEAD_DIM), BF16),
            pltpu.VMEM((ncp, 2 * HEAD_DIM), BF16),
            pltpu.VMEM((seq // tk, HEAD_DIM, tk), BF16),
            pltpu.VMEM((seq // tq, HEAD_DIM, tq), BF16),
            pltpu.VMEM((HEAD_DIM, ncp), BF16),
            pltpu.VMEM((tk, HPG * tq), F32),
        ],
        compiler_params=_params(3, 5 * seq * HEAD_DIM * 2, 3 * seq * HEAD_DIM * 2, HPG * tq * tk * 12),
        name="nsa_attention",
    )(slopes, z, kv_cmp, kv_cmp, z, z, z, z, gates, jnp.asarray(ovt, BF16), aug, augc)


def _ab_layer(h, bsz, seq, ln_g, w_in, conv_w, conv_b, w_r, b_r, w_i, b_i, lam, pool_w, pool_scale, w_out):
    d = h.shape[1]
    z = _norm_matmul(h, ln_g, w_in.astype(BF16), F32, COL_TILE)
    y = _mixer(z.reshape(bsz, seq, -1), conv_w, conv_b, w_r, b_r, w_i, b_i, lam, pool_w, pool_scale)
    return _matmul_residual(y.reshape(bsz * seq, -1), w_out.astype(BF16), h)


def _nsa_layer(h, bsz, seq, ln_g, w_in, pos_k, w1_k, w2_k, pos_v, w1_v, w2_v, w_out):
    d = h.shape[1]
    n_main = N_HEADS * HEAD_DIM + 6 * N_KV * HEAD_DIM
    n_gate = N_BRANCH * N_HEADS
    z = _norm_matmul(h, ln_g, w_in[:, :n_main].astype(BF16), BF16, COL_TILE)
    w_gate = jnp.pad(w_in[:, n_main:], ((0, 0), (0, LANES - n_gate))).astype(BF16)
    gl = _norm_matmul(h, ln_g, w_gate, F32, LANES)
    z = z.reshape(bsz, seq, n_main)

    nc = seq // CMP_STRIDE
    c0 = N_HEADS * HEAD_DIM
    zc = z[:, :, c0:c0 + 2 * N_KV * HEAD_DIM].reshape(bsz, nc, CMP_STRIDE, 2, N_KV, HEAD_DIM)
    xflat = zc.transpose(3, 0, 4, 1, 2, 5).reshape(2, bsz, N_KV, nc, CMP_STRIDE * HEAD_DIM)
    half = CMP_LEN // CMP_STRIDE
    pos = jnp.stack([pos_k, pos_v]).reshape(2, half, 1, CMP_STRIDE * HEAD_DIM)
    w1 = jnp.stack([w1_k, w1_v]).reshape(2, half, CMP_STRIDE * HEAD_DIM, -1).astype(BF16)
    w2 = jnp.stack([w2_k, w2_v]).astype(BF16)
    kv_cmp = _compress(xflat, pos, w1, w2)

    gates = gl[:, :n_gate].reshape(bsz, seq, N_BRANCH, N_KV, HPG).transpose(0, 3, 2, 4, 1)
    gates = jnp.pad(gates.reshape(bsz, N_KV, N_BRANCH * HPG, seq), ((0, 0), (0, 0), (0, 16 - N_BRANCH * HPG), (0, 0)))

    o = _nsa_attention(z, kv_cmp, gates)
    return _matmul_residual(o.reshape(bsz * seq, -1), w_out.astype(BF16), h)


def kernel(x, p, ln_mix_g, ab_w_in, ab_conv_w, ab_conv_b, ab_w_rgate, ab_b_rgate, ab_w_igate, ab_b_igate, ab_lambda, ab_pool_w, ab_pool_scale, ab_w_out, c_w_in, c_cmp_pos_k, c_cmp_w1_k, c_cmp_w2_k, c_cmp_pos_v, c_cmp_w1_v, c_cmp_w2_v, c_w_out, ln_mlp_g, mlp_w_up, mlp_w_down, ln_ple_g, ple_w_gate, ple_w_proj, ln_final_g):
    bsz, seq, d = x.shape
    depth = p.shape[0]
    h = x.reshape(bsz * seq, d)
    p_all = p.reshape(depth, bsz * seq, -1)
    for i in range(depth):
        j = i // 2
        if i % 2 == 0:
            h = _ab_layer(
                h, bsz, seq, ln_mix_g[i], ab_w_in[j], ab_conv_w[j], ab_conv_b[j], ab_w_rgate[j], ab_b_rgate[j],
                ab_w_igate[j], ab_b_igate[j], ab_lambda[j], ab_pool_w[j], ab_pool_scale[j], ab_w_out[j],
            )
        else:
            h = _nsa_layer(
                h, bsz, seq, ln_mix_g[i], c_w_in[j], c_cmp_pos_k[j], c_cmp_w1_k[j], c_cmp_w2_k[j],
                c_cmp_pos_v[j], c_cmp_w1_v[j], c_cmp_w2_v[j], c_w_out[j],
            )
        h = _mlp(h, ln_mlp_g[i], mlp_w_up[i].astype(BF16), mlp_w_down[i].astype(BF16))
        h = _ple(
            h, ln_ple_g[i], ple_w_gate[i].astype(BF16), p_all, i, ple_w_proj[i].astype(BF16),
            ln_final_g, final_norm=(i == depth - 1),
        )
    return h.reshape(bsz, seq, d)
```

```python
import functools

import jax
import jax.numpy as jnp
import numpy as np
from jax import lax
from jax.experimental import pallas as pl
from jax.experimental.pallas import tpu as pltpu

F32 = jnp.float32
BF16 = jnp.bfloat16

EPS = 1e-6
NEG = -1e30
FORCE_BONUS = 1e6
LRU_C = 8.0

LANES = 128
V7X_SCOPED_VMEM_CAP = 60000 * 1024
COMPILER_TEMP_BYTES = 12 * 1024 * 1024

LRU_HEADS = 16
POOL_WINDOWS = (2, 4, 8, 16)
CONV_WIDTH = 4
N_HEADS = 16
N_KV = 4
HPG = N_HEADS // N_KV
HEAD_DIM = 128
CMP_LEN = 32
CMP_STRIDE = 16
SEL_LEN = 64
SEL_TOPN = 16
WINDOW = 512
N_BRANCH = 3

ROW_TILE = 1024
COL_TILE = 1024
MLP_ROW_TILE = 512
MLP_FF_TILE = 1024
PLE_ROW_TILE = 512
MIX_TIME_TILE = 256
ATT_Q_TILE = 256
ATT_K_TILE = 512


def _params(n_axes, *block_bytes):
    need = 2 * sum(block_bytes) + COMPILER_TEMP_BYTES
    return pltpu.CompilerParams(
        dimension_semantics=("arbitrary",) * n_axes,
        vmem_limit_bytes=int(min(need, V7X_SCOPED_VMEM_CAP)),
    )


def _rms_bf16(x, g):
    ms = jnp.mean(x * x, axis=-1, keepdims=True)
    return (x * lax.rsqrt(ms + EPS) * g).astype(BF16)


def _sigmoid(x):
    return 1.0 / (1.0 + jnp.exp(-x))


def _gelu_tanh(x):
    c = np.float32(np.sqrt(2.0 / np.pi))
    return x * (0.5 * (1.0 + jnp.tanh(c * (x + 0.044715 * (x * x * x)))))


def _dot(a, b):
    return jnp.dot(a, b, preferred_element_type=F32)


def _dot_nt(a, b):
    return lax.dot_general(a, b, (((1,), (1,)), ((), ())), preferred_element_type=F32)


def _norm_matmul_kernel(x_ref, g_ref, w_ref, o_ref, xn_ref):
    @pl.when(pl.program_id(1) == 0)
    def _():
        xn_ref[...] = _rms_bf16(x_ref[...], g_ref[...])

    o_ref[...] = _dot(xn_ref[...], w_ref[...]).astype(o_ref.dtype)


def _norm_matmul(x, g, w_all, layer, out_dtype):
    m, k = x.shape
    n = w_all.shape[2]
    tm = min(ROW_TILE, m)
    tn = min(COL_TILE, n)
    osz = jnp.dtype(out_dtype).itemsize
    return pl.pallas_call(
        _norm_matmul_kernel,
        grid=(m // tm, n // tn),
        in_specs=[
            pl.BlockSpec((tm, k), lambda i, j: (i, 0)),
            pl.BlockSpec((1, k), lambda i, j: (0, 0)),
            pl.BlockSpec((None, k, tn), lambda i, j: (layer, 0, j)),
        ],
        out_specs=pl.BlockSpec((tm, tn), lambda i, j: (i, j)),
        out_shape=jax.ShapeDtypeStruct((m, n), out_dtype),
        scratch_shapes=[pltpu.VMEM((tm, k), BF16)],
        compiler_params=_params(2, tm * k * 4, k * tn * 2, tm * tn * osz, tm * k),
        name="norm_matmul",
    )(x, g.reshape(1, k), w_all)


def _nsa_proj_kernel(x_ref, g_ref, w_ref, wg_ref, z_ref, gl_ref, xn_ref):
    @pl.when(pl.program_id(1) == 0)
    def _():
        xn_ref[...] = _rms_bf16(x_ref[...], g_ref[...])
        gl_ref[...] = _dot(xn_ref[...], wg_ref[...])

    r = _dot(xn_ref[...], w_ref[...]).astype(z_ref.dtype)
    for hb in range(z_ref.shape[0]):
        z_ref[hb] = r[:, hb * HEAD_DIM:(hb + 1) * HEAD_DIM]


def _nsa_proj(x, g, w_all, layer, w_gate, bsz, seq):
    m, k = x.shape
    n_main = N_HEADS * HEAD_DIM + 6 * N_KV * HEAD_DIM
    tm = min(ROW_TILE, seq)
    tn = COL_TILE
    hpb = tn // HEAD_DIM
    spb = seq // tm
    return pl.pallas_call(
        _nsa_proj_kernel,
        grid=(m // tm, n_main // tn),
        in_specs=[
            pl.BlockSpec((tm, k), lambda i, j: (i, 0)),
            pl.BlockSpec((1, k), lambda i, j: (0, 0)),
            pl.BlockSpec((None, k, tn), lambda i, j: (layer, 0, j)),
            pl.BlockSpec((k, LANES), lambda i, j: (0, 0)),
        ],
        out_specs=[
            pl.BlockSpec((None, hpb, tm, HEAD_DIM), lambda i, j: (i // spb, j, i % spb, 0)),
            pl.BlockSpec((tm, LANES), lambda i, j: (i, 0)),
        ],
        out_shape=[
            jax.ShapeDtypeStruct((bsz, n_main // HEAD_DIM, seq, HEAD_DIM), BF16),
            jax.ShapeDtypeStruct((m, LANES), F32),
        ],
        scratch_shapes=[pltpu.VMEM((tm, k), BF16)],
        compiler_params=_params(2, tm * k * 4, k * tn * 2, k * LANES * 2, tm * tn * 2, tm * LANES * 4, tm * k),
        name="nsa_proj",
    )(x, g.reshape(1, k), w_all, w_gate)


def _matmul_residual_kernel(y_ref, w_ref, h_ref, o_ref):
    o_ref[...] = h_ref[...] + _dot(y_ref[...], w_ref[...])


def _matmul_residual(y, w_all, layer, h):
    m, k = y.shape
    n = w_all.shape[2]
    tm = min(ROW_TILE, m)
    tn = min(COL_TILE, n)
    return pl.pallas_call(
        _matmul_residual_kernel,
        grid=(m // tm, n // tn),
        in_specs=[
            pl.BlockSpec((tm, k), lambda i, j: (i, 0)),
            pl.BlockSpec((None, k, tn), lambda i, j: (layer, 0, j)),
            pl.BlockSpec((tm, tn), lambda i, j: (i, j)),
        ],
        out_specs=pl.BlockSpec((tm, tn), lambda i, j: (i, j)),
        out_shape=jax.ShapeDtypeStruct((m, n), F32),
        compiler_params=_params(2, tm * k * 2, k * tn * 2, tm * tn * 4, tm * tn * 4),
        name="matmul_residual",
    )(y, w_all, h)


def _mlp_kernel(x_ref, g_ref, wu_ref, wd_ref, o_ref, xn_ref, acc_ref):
    j = pl.program_id(1)

    @pl.when(j == 0)
    def _():
        xn_ref[...] = _rms_bf16(x_ref[...], g_ref[...])
        acc_ref[...] = jnp.zeros_like(acc_ref)

    u = _dot(xn_ref[...], wu_ref[...])
    u = jnp.square(jnp.maximum(u, 0.0)).astype(BF16)
    acc_ref[...] += _dot(u, wd_ref[...])

    @pl.when(j == pl.num_programs(1) - 1)
    def _():
        o_ref[...] = x_ref[...] + acc_ref[...]


def _mlp(x, g, w_up, w_down, layer):
    m, d = x.shape
    f = w_up.shape[2]
    tm = min(MLP_ROW_TILE, m)
    tf = min(MLP_FF_TILE, f)
    return pl.pallas_call(
        _mlp_kernel,
        grid=(m // tm, f // tf),
        in_specs=[
            pl.BlockSpec((tm, d), lambda i, j: (i, 0)),
            pl.BlockSpec((1, d), lambda i, j: (0, 0)),
            pl.BlockSpec((None, d, tf), lambda i, j: (layer, 0, j)),
            pl.BlockSpec((None, tf, d), lambda i, j: (layer, j, 0)),
        ],
        out_specs=pl.BlockSpec((tm, d), lambda i, j: (i, 0)),
        out_shape=jax.ShapeDtypeStruct((m, d), F32),
        scratch_shapes=[pltpu.VMEM((tm, d), BF16), pltpu.VMEM((tm, d), F32)],
        compiler_params=_params(2, tm * d * 4, d * tf * 2, tf * d * 2, tm * d * 4, tm * d * 3),
        name="mlp",
    )(x, g.reshape(1, d), w_up, w_down)


def _ple_kernel(x_ref, g_ref, wg_ref, p_ref, wp_ref, gf_ref, o_ref, *, final_norm):
    x = x_ref[...]
    gate = _sigmoid(_dot(_rms_bf16(x, g_ref[...]), wg_ref[...]))
    proj = _dot(p_ref[...].astype(BF16), wp_ref[...])
    h = x + gate * proj
    if final_norm:
        ms = jnp.mean(h * h, axis=-1, keepdims=True)
        h = h * lax.rsqrt(ms + EPS) * gf_ref[...]
    o_ref[...] = h


def _ple(x, g, w_gate, p_all, layer, w_proj, g_final, final_norm):
    m, d = x.shape
    e = p_all.shape[-1]
    tm = min(PLE_ROW_TILE, m)
    const = lambda shape: pl.BlockSpec(shape, lambda i: (0,) * len(shape))
    return pl.pallas_call(
        functools.partial(_ple_kernel, final_norm=final_norm),
        grid=(m // tm,),
        in_specs=[
            pl.BlockSpec((tm, d), lambda i: (i, 0)),
            const((1, d)),
            pl.BlockSpec((None, d, d), lambda i: (layer, 0, 0)),
            pl.BlockSpec((None, tm, e), lambda i: (layer, i, 0)),
            pl.BlockSpec((None, e, d), lambda i: (layer, 0, 0)),
            const((1, d)),
        ],
        out_specs=pl.BlockSpec((tm, d), lambda i: (i, 0)),
        out_shape=jax.ShapeDtypeStruct((m, d), F32),
        compiler_params=_params(1, tm * d * 4, d * d * 2, tm * e * 4, e * d * 2, tm * d * 4, tm * d * 6),
        name="ple",
    )(x, g.reshape(1, d), w_gate, p_all, w_proj, g_final.reshape(1, d))


def _mixer_kernel(
    z_ref, cw_ref, cb_ref, wr_ref, br_ref, wi_ref, bi_ref, lam_ref, pw_ref, ps_ref,
    y_ref, xext_ref, uext_ref, hlast_ref, *, ts, width,
):
    s = pl.program_id(1)
    xh = 8
    uh = 16

    @pl.when(s == 0)
    def _():
        xext_ref[0:xh, :] = jnp.zeros((xh, width), F32)
        uext_ref[0:uh, :] = jnp.zeros((uh, width), F32)
        hlast_ref[...] = jnp.zeros_like(hlast_ref)

    xr = z_ref[:, 0:width]
    gate = z_ref[:, width:2 * width]
    u = z_ref[:, 2 * width:3 * width]

    xext_ref[xh:xh + ts, :] = xr
    conv = cb_ref[...] + cw_ref[CONV_WIDTH - 1:CONV_WIDTH, :] * xr
    for k in range(CONV_WIDTH - 1):
        shift = CONV_WIDTH - 1 - k
        conv = conv + cw_ref[k:k + 1, :] * xext_ref[pl.ds(xh - shift, ts), :]
    xext_ref[0:xh, :] = xr[ts - xh:ts, :]

    gw = wr_ref.shape[1]
    conv_b = conv.astype(BF16)
    r_parts, i_parts = [], []
    for c in range(width // gw):
        xc = conv_b[:, c * gw:(c + 1) * gw]
        r_parts.append(_dot(xc, wr_ref[c]))
        i_parts.append(_dot(xc, wi_ref[c]))
    r = _sigmoid(jnp.concatenate(r_parts, axis=1) + br_ref[...])
    ig = _sigmoid(jnp.concatenate(i_parts, axis=1) + bi_ref[...])

    nl = -lam_ref[...]
    softplus = jnp.maximum(nl, 0.0) + jnp.log(1.0 + jnp.exp(-jnp.abs(nl)))
    log_a = -LRU_C * r * softplus
    a = jnp.exp(log_a)
    b = jnp.sqrt(1.0 - a * a) * (ig * conv)

    sub = 8
    srow = lax.broadcasted_iota(jnp.int32, (sub, 1), 0)
    carry = hlast_ref[...]
    h_groups = []
    for r0 in range(0, ts, sub):
        ag = a[r0:r0 + sub, :]
        bg = b[r0:r0 + sub, :]
        d = 1
        while d < sub:
            keep = srow >= d
            a_sh = jnp.where(keep, pltpu.roll(ag, d, axis=0), 1.0)
            b_sh = jnp.where(keep, pltpu.roll(bg, d, axis=0), 0.0)
            bg = ag * b_sh + bg
            ag = ag * a_sh
            d *= 2
        hg = ag * carry + bg
        carry = hg[sub - 1:sub, :]
        h_groups.append(hg)
    h = jnp.concatenate(h_groups, axis=0)
    hlast_ref[...] = carry
    y_ref[:, 0:width] = (h * _gelu_tanh(gate)).astype(BF16)

    uext_ref[uh:uh + ts, :] = u
    t = s * ts + lax.broadcasted_iota(jnp.int32, (ts, 1), 0)
    gd = width // len(POOL_WINDOWS)
    for gi, win in enumerate(POOL_WINDOWS):
        cs = slice(gi * gd, (gi + 1) * gd)
        wsum = u[:, cs]
        for k in range(1, win):
            wsum = wsum + uext_ref[pl.ds(uh - k, ts), cs]
        cnt = jnp.minimum(t + 1, win).astype(F32)
        dd = wsum / cnt - u[:, cs]
        yp = _dot(dd.astype(BF16), pw_ref[gi]) * ps_ref[:, cs]
        y_ref[:, width + gi * gd:width + (gi + 1) * gd] = yp.astype(BF16)
    uext_ref[0:uh, :] = u[ts - uh:ts, :]


def _mixer(z, conv_w, conv_b, w_r, b_r, w_i, b_i, lam, pool_w, pool_scale):
    bsz, seq, w3 = z.shape
    width = w3 // 3
    ts = min(MIX_TIME_TILE, seq)
    hd = width // LRU_HEADS
    per = 2 * LANES // hd
    gw = per * hd

    def block_diag(w):
        w = w.reshape(LRU_HEADS // per, per, hd, hd)
        eye = jnp.eye(per, dtype=w.dtype)
        return jnp.einsum("cpij,pq->cpiqj", w, eye).reshape(LRU_HEADS // per, gw, gw).astype(BF16)

    row = lambda v: v.reshape(1, width)
    const2 = lambda shape: pl.BlockSpec(shape, lambda b, s: (0,) * len(shape))
    kern = functools.partial(_mixer_kernel, ts=ts, width=width)
    gd = width // len(POOL_WINDOWS)
    return pl.pallas_call(
        kern,
        grid=(bsz, seq // ts),
        in_specs=[
            pl.BlockSpec((None, ts, w3), lambda b, s: (b, s, 0)),
            const2((CONV_WIDTH, width)),
            const2((1, width)),
            const2((LRU_HEADS // per, gw, gw)),
            const2((1, width)),
            const2((LRU_HEADS // per, gw, gw)),
            const2((1, width)),
            const2((1, width)),
            const2((len(POOL_WINDOWS), gd, gd)),
            const2((1, width)),
        ],
        out_specs=pl.BlockSpec((None, ts, 2 * width), lambda b, s: (b, s, 0)),
        out_shape=jax.ShapeDtypeStruct((bsz, seq, 2 * width), BF16),
        scratch_shapes=[
            pltpu.VMEM((ts + 8, width), F32),
            pltpu.VMEM((ts + 16, width), F32),
            pltpu.VMEM((1, width), F32),
        ],
        compiler_params=_params(2, ts * w3 * 4, ts * width * 4, ts * width * 16),
        name="lru_pool_mixer",
    )(
        z, conv_w, row(conv_b), block_diag(w_r), row(b_r), block_diag(w_i), row(b_i), row(lam),
        pool_w.astype(BF16), row(pool_scale),
    )


def _compress_kernel(x_ref, pos_ref, w1_ref, w2_ref, o_ref):
    x = x_ref[...].astype(F32)
    xa = (x + pos_ref[0]).astype(BF16)
    xb = (x + pos_ref[1]).astype(BF16)
    first = _dot(xa, w1_ref[0])
    second = _dot(xb, w1_ref[1])
    n = x.shape[0]
    hid = first + pltpu.roll(second, n - 1, axis=0)
    o_ref[...] = _dot(_gelu_tanh(hid).astype(BF16), w2_ref[...]).astype(o_ref.dtype)


def _compress(zc, pos, w1, w2):
    bsz, _, nc, kk = zc.shape
    two, ng = 2, N_KV
    hid = w1.shape[-1]
    dh = w2.shape[-1]
    return pl.pallas_call(
        _compress_kernel,
        grid=(two, bsz, ng),
        in_specs=[
            pl.BlockSpec((None, None, nc, kk), lambda t, b, g: (b, N_HEADS + t * N_KV + g, 0, 0)),
            pl.BlockSpec((None, 2, 1, kk), lambda t, b, g: (t, 0, 0, 0)),
            pl.BlockSpec((None, 2, kk, hid), lambda t, b, g: (t, 0, 0, 0)),
            pl.BlockSpec((None, hid, dh), lambda t, b, g: (t, 0, 0)),
        ],
        out_specs=pl.BlockSpec((None, None, None, nc, dh), lambda t, b, g: (t, b, g, 0, 0)),
        out_shape=jax.ShapeDtypeStruct((two, bsz, ng, nc, dh), BF16),
        compiler_params=_params(3, nc * kk * 2, 2 * kk * hid * 2, hid * dh * 2, nc * dh * 2, nc * kk * 8),
        name="nsa_compress",
    )(zc, pos, w1, w2)


LOG2E = np.float32(1.4426950408889634)
AUG_MASK_ROWS = 64
ALIBI_PARTS = 3


def _t32(x):
    r, c = x.shape
    rows = []
    for cb in range(0, c, LANES):
        rows.append(jnp.concatenate([x[rb:rb + LANES, cb:cb + LANES].T for rb in range(0, r, LANES)], axis=1))
    return jnp.concatenate(rows, axis=0)


def _t_bf16(x):
    return _t32(x.astype(F32)).astype(BF16)


def _nsa_kernel(
    slope_ref, q_ref, kc_ref, vc_ref, ks_ref, vs_ref, kw_ref, vw_ref, gt_ref, ovt_ref, aug_ref, augc_ref,
    o_ref, kp_ref, kwp_ref, kcp_ref, vst_ref, vwt_ref, vct_ref, s_ref, gsc_ref, *, tq, tk, seq,
):
    g = pl.program_id(1)
    i = pl.program_id(2)
    t0 = i * tq
    nl = HPG * tq
    n_sel = seq // SEL_LEN
    ncp = kc_ref.shape[0]

    @pl.when(i == 0)
    def _():
        kp_ref[:, 0:HEAD_DIM] = ks_ref[...]
        kp_ref[:, HEAD_DIM:2 * HEAD_DIM] = aug_ref[...]
        kwp_ref[:, 0:HEAD_DIM] = kw_ref[...]
        kwp_ref[:, HEAD_DIM:2 * HEAD_DIM] = aug_ref[...]
        kcp_ref[:, 0:HEAD_DIM] = kc_ref[...]
        kcp_ref[:, HEAD_DIM:2 * HEAD_DIM] = augc_ref[...]
        for kt in range(seq // tk):
            vst_ref[kt] = _t_bf16(vs_ref[kt * tk:(kt + 1) * tk, :])
        for kb in range(seq // tq):
            vwt_ref[kb] = _t_bf16(vw_ref[kb * tq:(kb + 1) * tq, :])
        vct_ref[...] = _t_bf16(vc_ref[...])

    qscale = np.float32(HEAD_DIM ** -0.5) * LOG2E
    q_t = jnp.concatenate(
        [_t32(q_ref[h].astype(F32) * qscale) for h in range(HPG)], axis=1
    ).astype(BF16)

    lane = lax.broadcasted_iota(jnp.int32, (1, nl), 1)
    slope = jnp.zeros((1, nl), F32)
    for h in range(HPG):
        in_h = (lane >= h * tq) & (lane < (h + 1) * tq)
        slope = jnp.where(in_h, slope_ref[g * HPG + h] * LOG2E, slope)
    t_row = t0 + (lane & (tq - 1))

    arow = lax.broadcasted_iota(jnp.int32, (LANES - AUG_MASK_ROWS, nl), 0)
    alibi_rows = jnp.zeros((LANES - AUG_MASK_ROWS, nl), F32)
    rem = slope
    for k in range(ALIBI_PARTS):
        piece = rem.astype(BF16).astype(F32)
        rem = rem - piece
        alibi_rows = jnp.where(arow == 2 * k, piece * float(SEL_LEN), alibi_rows)
        alibi_rows = jnp.where(arow == 2 * k + 1, piece, alibi_rows)
    qa_plain = jnp.concatenate(
        [q_t, jnp.zeros((AUG_MASK_ROWS, nl), BF16), alibi_rows.astype(BF16)], axis=0
    )

    cend = lax.broadcasted_iota(jnp.int32, (ncp, 1), 0) * CMP_STRIDE + (CMP_LEN - 1)
    valid_c = cend <= t_row
    s_c = jnp.where(valid_c, _dot(kcp_ref[...], qa_plain), NEG)
    m_c = jnp.max(s_c, axis=0, keepdims=True)
    p_c = jnp.where(valid_c, jnp.exp2(s_c - m_c), 0.0)
    l_c = jnp.sum(p_c, axis=0, keepdims=True)
    p_c = p_c * (1.0 / jnp.where(l_c > 0.0, l_c, 1.0))
    o_c = _dot(vct_ref[...], p_c.astype(BF16))

    p_sum = p_c[:, 0:tq]
    for h in range(1, HPG):
        p_sum = p_sum + p_c[:, h * tq:(h + 1) * tq]
    ovt = ovt_ref[...]
    top_n = min(SEL_TOPN, n_sel)
    sub = 8
    mask_cols = []
    for c0 in range(0, tq, LANES):
        imp = jnp.zeros((n_sel, LANES), F32)
        rem = p_sum[:, c0:c0 + LANES]
        for _ in range(3):
            piece = rem.astype(BF16)
            rem = rem - piece.astype(F32)
            imp = imp + _dot(ovt, piece)
        blk = lax.broadcasted_iota(jnp.int32, (n_sel, LANES), 0)
        tl = t0 + c0 + lax.broadcasted_iota(jnp.int32, (n_sel, LANES), 1)
        cur = lax.shift_right_arithmetic(tl, int(np.log2(SEL_LEN)))
        forced = (blk == 0) | (blk == cur) | (blk == cur - 1)
        causal_s = blk * SEL_LEN <= tl
        v = jnp.where(forced, imp + FORCE_BONUS, imp)
        v = jnp.where(causal_s, v, NEG)
        rank_slabs = []
        for r0 in range(0, n_sel, sub):
            vs_ = v[r0:r0 + sub, :]
            lower = blk[r0:r0 + sub, :]
            rk = jnp.zeros((sub, LANES), F32)
            for j in range(n_sel):
                vj = v[j:j + 1, :]
                if j < r0:
                    beats = vj >= vs_
                elif j >= r0 + sub - 1:
                    beats = vj > vs_
                else:
                    beats = (vj > vs_) | ((vj == vs_) & (lower > j))
                rk = rk + jnp.where(beats, 1.0, 0.0)
            rank_slabs.append(rk)
        rank = jnp.concatenate(rank_slabs, axis=0)
        mask_cols.append(jnp.where((rank < top_n) & causal_s, 0.0, NEG))
    mask_rows = mask_cols[0] if len(mask_cols) == 1 else jnp.concatenate(mask_cols, axis=1)
    if n_sel < AUG_MASK_ROWS:
        mask_rows = jnp.concatenate([mask_rows, jnp.zeros((AUG_MASK_ROWS - n_sel, tq), F32)], axis=0)
    mask_rows = jnp.concatenate([mask_rows] * HPG, axis=1).astype(BF16)
    qa_sel = jnp.concatenate([q_t, mask_rows, alibi_rows.astype(BF16)], axis=0)

    span = WINDOW + tq
    start = pl.multiple_of(jnp.maximum(t0 - WINDOW, 0), tq)
    dist = t_row - (start + lax.broadcasted_iota(jnp.int32, (span, 1), 0))
    valid_w = lax.bitcast_convert_type(dist, jnp.uint32) < jnp.uint32(WINDOW)
    s_w = jnp.where(valid_w, _dot(kwp_ref[pl.ds(start, span), :], qa_plain), NEG)
    m_w = jnp.max(s_w, axis=0, keepdims=True)
    p_w = jnp.exp2(s_w - m_w)
    l_w = jnp.sum(p_w, axis=0, keepdims=True)
    p_wb = p_w.astype(BF16)
    sb = start // tq
    acc_w = jnp.zeros((HEAD_DIM, nl), F32)
    for k in range(span // tq):
        acc_w = acc_w + _dot(vwt_ref[sb + k], p_wb[k * tq:(k + 1) * tq, :])

    krow = lax.broadcasted_iota(jnp.int32, (tk, 1), 0)

    def score_tile(kt, masked):
        j0 = pl.multiple_of(kt * tk, tk)
        s = _dot(kp_ref[pl.ds(j0, tk), :], qa_sel)
        if masked:
            s = jnp.where(j0 + krow <= t_row, s, NEG)
        s_ref[...] = s
        return jnp.max(s, axis=0, keepdims=True)

    def finish_tile(kt, m_p, l_p, acc, m_t):
        m_n = jnp.maximum(m_p, m_t)
        p = jnp.exp2(s_ref[...] - m_n)
        alpha = jnp.exp2(m_p - m_n)
        l_n = alpha * l_p + jnp.sum(p, axis=0, keepdims=True)
        acc = alpha * acc + _dot(vst_ref[kt], p.astype(BF16))
        return m_n, l_n, acc

    n_kt = (t0 + tq - 1) // tk + 1

    def pipe_step(kt, carry):
        m_p, l_p, acc, m_t, prev = carry
        m_n, l_n, acc = finish_tile(prev, m_p, l_p, acc, m_t)
        return m_n, l_n, acc, score_tile(kt, False), kt

    carry = (
        jnp.full((1, nl), NEG, F32), jnp.zeros((1, nl), F32), jnp.zeros((HEAD_DIM, nl), F32),
        score_tile(n_kt - 1, True), n_kt - 1,
    )
    m_p, l_p, acc, m_t, prev = lax.fori_loop(0, n_kt - 1, pipe_step, carry)
    _, l_s, acc_s = finish_tile(prev, m_p, l_p, acc, m_t)

    gsc_ref[...] = _t32(_sigmoid(gt_ref[...]))

    def gate_row(br):
        first = br * N_HEADS + g * HPG
        return jnp.concatenate([gsc_ref[pl.ds(first + h, 1), :] for h in range(HPG)], axis=1)

    o_t = (
        gate_row(0) * o_c
        + (gate_row(1) * (1.0 / l_s)) * acc_s
        + (gate_row(2) * (1.0 / l_w)) * acc_w
    )
    for h in range(HPG):
        o_ref[:, h * HEAD_DIM:(h + 1) * HEAD_DIM] = _t32(o_t[:, h * tq:(h + 1) * tq]).astype(o_ref.dtype)


def _nsa_attention(z, kv_cmp, gates):
    bsz, _, seq, _ = z.shape
    tq = min(ATT_Q_TILE, seq)
    tk = min(ATT_K_TILE, seq)
    n_sel = seq // SEL_LEN
    ncp = kv_cmp.shape[3]
    qw = HPG * HEAD_DIM
    kv0 = N_HEADS
    assert n_sel <= AUG_MASK_ROWS and tq & (tq - 1) == 0 and seq % tk == 0 and tk % tq == 0

    cs = np.arange(ncp)[None, :] * CMP_STRIDE
    ss = np.arange(n_sel)[:, None] * SEL_LEN
    ovt = ((cs < ss + SEL_LEN) & (cs + CMP_LEN > ss) & (np.arange(ncp)[None, :] < ncp - 1)).astype(np.float32)

    def aug_cols(pos):
        a = np.zeros((pos.shape[0], LANES), np.float32)
        a[np.arange(pos.shape[0]), np.minimum(pos // SEL_LEN, AUG_MASK_ROWS - 1)] = 1.0
        for k in range(ALIBI_PARTS):
            a[:, AUG_MASK_ROWS + 2 * k] = pos // SEL_LEN
            a[:, AUG_MASK_ROWS + 2 * k + 1] = pos % SEL_LEN
        return jnp.asarray(a, BF16)

    aug = aug_cols(np.arange(seq))
    augc = aug_cols(np.arange(ncp) * CMP_STRIDE + CMP_LEN - 1)
    slopes = jnp.exp2(-8.0 * jnp.arange(1, N_HEADS + 1, dtype=F32) / N_HEADS)

    def kv_spec(col):
        return pl.BlockSpec((None, None, seq, HEAD_DIM), lambda b, g, i, col=col: (b, kv0 + col * N_KV + g, 0, 0))

    const = lambda shape: pl.BlockSpec(shape, lambda b, g, i: (0,) * len(shape))
    kern = functools.partial(_nsa_kernel, tq=tq, tk=tk, seq=seq)
    return pl.pallas_call(
        kern,
        grid=(bsz, N_KV, seq // tq),
        in_specs=[
            pl.BlockSpec(memory_space=pltpu.SMEM),
            pl.BlockSpec((None, HPG, tq, HEAD_DIM), lambda b, g, i: (b, g, i, 0)),
            pl.BlockSpec((None, None, None, ncp, HEAD_DIM), lambda b, g, i: (0, b, g, 0, 0)),
            pl.BlockSpec((None, None, None, ncp, HEAD_DIM), lambda b, g, i: (1, b, g, 0, 0)),
            kv_spec(2), kv_spec(3), kv_spec(4), kv_spec(5),
            pl.BlockSpec((None, tq, LANES), lambda b, g, i: (b, i, 0)),
            const((n_sel, ncp)),
            const((seq, LANES)),
            const((ncp, LANES)),
        ],
        out_specs=pl.BlockSpec((None, tq, qw), lambda b, g, i: (b, i, g)),
        out_shape=jax.ShapeDtypeStruct((bsz, seq, N_HEADS * HEAD_DIM), BF16),
        scratch_shapes=[
            pltpu.VMEM((seq, 2 * HEAD_DIM), BF16),
            pltpu.VMEM((seq, 2 * HEAD_DIM), BF16),
            pltpu.VMEM((ncp, 2 * HEAD_DIM), BF16),
            pltpu.VMEM((seq // tk, HEAD_DIM, tk), BF16),
            pltpu.VMEM((seq // tq, HEAD_DIM, tq), BF16),
            pltpu.VMEM((HEAD_DIM, ncp), BF16),
            pltpu.VMEM((tk, HPG * tq), F32),
            pltpu.VMEM((LANES, tq), F32),
        ],
        compiler_params=_params(3, 5 * seq * HEAD_DIM * 2, 3 * seq * HEAD_DIM * 2, HPG * tq * tk * 12),
        name="nsa_attention",
    )(slopes, z, kv_cmp, kv_cmp, z, z, z, z, gates, jnp.asarray(ovt, BF16), aug, augc)


def _ab_layer(h, bsz, seq, j, ln_g, w_in, conv_w, conv_b, w_r, b_r, w_i, b_i, lam, pool_w, pool_scale, w_out):
    z = _norm_matmul(h, ln_g, w_in, j, F32)
    y = _mixer(z.reshape(bsz, seq, -1), conv_w, conv_b, w_r, b_r, w_i, b_i, lam, pool_w, pool_scale)
    return _matmul_residual(y.reshape(bsz * seq, -1), w_out, j, h)


def _nsa_layer(h, bsz, seq, j, ln_g, w_in, w_gate, pos_k, w1_k, w2_k, pos_v, w1_v, w2_v, w_out):
    z, gl = _nsa_proj(h, ln_g, w_in, j, w_gate, bsz, seq)

    nc = seq // CMP_STRIDE
    zc = z.reshape(bsz, z.shape[1], nc, CMP_STRIDE * HEAD_DIM)
    half = CMP_LEN // CMP_STRIDE
    pos = jnp.stack([pos_k, pos_v]).reshape(2, half, 1, CMP_STRIDE * HEAD_DIM)
    w1 = jnp.stack([w1_k, w1_v]).reshape(2, half, CMP_STRIDE * HEAD_DIM, -1).astype(BF16)
    w2 = jnp.stack([w2_k, w2_v]).astype(BF16)
    kv_cmp = _compress(zc, pos, w1, w2)

    o = _nsa_attention(z, kv_cmp, gl.reshape(bsz, seq, LANES))
    return _matmul_residual(o.reshape(bsz * seq, -1), w_out, j, h)


def kernel(x, p, ln_mix_g, ab_w_in, ab_conv_w, ab_conv_b, ab_w_rgate, ab_b_rgate, ab_w_igate, ab_b_igate, ab_lambda, ab_pool_w, ab_pool_scale, ab_w_out, c_w_in, c_cmp_pos_k, c_cmp_w1_k, c_cmp_w2_k, c_cmp_pos_v, c_cmp_w1_v, c_cmp_w2_v, c_w_out, ln_mlp_g, mlp_w_up, mlp_w_down, ln_ple_g, ple_w_gate, ple_w_proj, ln_final_g):
    bsz, seq, d = x.shape
    depth = p.shape[0]
    h = x.reshape(bsz * seq, d)
    p_all = p.reshape(depth, bsz * seq, -1)
    ab_in, ab_out = ab_w_in.astype(BF16), ab_w_out.astype(BF16)
    c_in, c_out = c_w_in.astype(BF16), c_w_out.astype(BF16)
    w_up, w_down = mlp_w_up.astype(BF16), mlp_w_down.astype(BF16)
    w_pg, w_pp = ple_w_gate.astype(BF16), ple_w_proj.astype(BF16)
    n_main = N_HEADS * HEAD_DIM + 6 * N_KV * HEAD_DIM
    n_gate = N_BRANCH * N_HEADS
    for i in range(depth):
        j = i // 2
        if i % 2 == 0:
            h = _ab_layer(
                h, bsz, seq, j, ln_mix_g[i], ab_in, ab_conv_w[j], ab_conv_b[j], ab_w_rgate[j], ab_b_rgate[j],
                ab_w_igate[j], ab_b_igate[j], ab_lambda[j], ab_pool_w[j], ab_pool_scale[j], ab_out,
            )
        else:
            w_gate = jnp.pad(c_w_in[j, :, n_main:], ((0, 0), (0, LANES - n_gate))).astype(BF16)
            h = _nsa_layer(
                h, bsz, seq, j, ln_mix_g[i], c_in, w_gate, c_cmp_pos_k[j], c_cmp_w1_k[j], c_cmp_w2_k[j],
                c_cmp_pos_v[j], c_cmp_w1_v[j], c_cmp_w2_v[j], c_out,
            )
        h = _mlp(h, ln_mlp_g[i], w_up, w_down, i)
        h = _ple(h, ln_ple_g[i], w_pg, p_all, i, w_pp, ln_final_g, final_norm=(i == depth - 1))
    return h.reshape(bsz, seq, d)
```

```python
import functools

import jax
import jax.numpy as jnp
import numpy as np
from jax import lax
from jax.experimental import pallas as pl
from jax.experimental.pallas import tpu as pltpu

F32 = jnp.float32
BF16 = jnp.bfloat16

EPS = 1e-6
NEG = -1e30
FORCE_BONUS = 1e6
LRU_C = 8.0

LANES = 128
V7X_SCOPED_VMEM_CAP = 60000 * 1024
COMPILER_TEMP_BYTES = 12 * 1024 * 1024

LRU_HEADS = 16
POOL_WINDOWS = (2, 4, 8, 16)
CONV_WIDTH = 4
N_HEADS = 16
N_KV = 4
HPG = N_HEADS // N_KV
HEAD_DIM = 128
CMP_LEN = 32
CMP_STRIDE = 16
SEL_LEN = 64
SEL_TOPN = 16
WINDOW = 512
N_BRANCH = 3

ROW_TILE = 1024
COL_TILE = 1024
MLP_ROW_TILE = 512
MLP_FF_TILE = 1024
PLE_ROW_TILE = 512
WIDE_ROW_TILE = 512
MIX_TIME_TILE = 256
ATT_Q_TILE = 256
ATT_K_TILE = 512


def _params(n_axes, *block_bytes):
    need = 2 * sum(block_bytes) + COMPILER_TEMP_BYTES
    return pltpu.CompilerParams(
        dimension_semantics=("arbitrary",) * n_axes,
        vmem_limit_bytes=int(min(need, V7X_SCOPED_VMEM_CAP)),
    )


def _rms_bf16(x, g):
    ms = jnp.mean(x * x, axis=-1, keepdims=True)
    return (x * lax.rsqrt(ms + EPS) * g).astype(BF16)


def _sigmoid(x):
    return 1.0 / (1.0 + jnp.exp(-x))


def _gelu_tanh(x):
    c = np.float32(np.sqrt(2.0 / np.pi))
    return x * (0.5 * (1.0 + jnp.tanh(c * (x + 0.044715 * (x * x * x)))))


def _dot(a, b):
    return jnp.dot(a, b, preferred_element_type=F32)


def _dot_nt(a, b):
    return lax.dot_general(a, b, (((1,), (1,)), ((), ())), preferred_element_type=F32)


def _norm_matmul_kernel(x_ref, g_ref, w_ref, o_ref):
    o_ref[...] = _dot(_rms_bf16(x_ref[...], g_ref[...]), w_ref[...]).astype(o_ref.dtype)


def _norm_matmul(x, g, w_all, layer, out_dtype):
    m, k = x.shape
    n = w_all.shape[2]
    tm = min(WIDE_ROW_TILE, m)
    osz = jnp.dtype(out_dtype).itemsize
    return pl.pallas_call(
        _norm_matmul_kernel,
        grid=(m // tm,),
        in_specs=[
            pl.BlockSpec((tm, k), lambda i: (i, 0)),
            pl.BlockSpec((1, k), lambda i: (0, 0)),
            pl.BlockSpec((None, k, n), lambda i: (layer, 0, 0)),
        ],
        out_specs=pl.BlockSpec((tm, n), lambda i: (i, 0)),
        out_shape=jax.ShapeDtypeStruct((m, n), out_dtype),
        compiler_params=_params(1, tm * k * 4, k * n * 2, tm * n * osz),
        name="norm_matmul",
    )(x, g.reshape(1, k), w_all)


def _nsa_proj_kernel(x_ref, g_ref, w_ref, wg_ref, z_ref, gl_ref, zc_ref, xn_ref, col_ref, *, cmp_step):
    j = pl.program_id(1)

    @pl.when(j == 0)
    def _():
        xn_ref[...] = _rms_bf16(x_ref[...], g_ref[...])
        gl_ref[...] = _dot(xn_ref[...], wg_ref[...])

    r = _dot(xn_ref[...], w_ref[...])
    rb = r.astype(z_ref.dtype)
    for hb in range(z_ref.shape[0]):
        z_ref[hb] = rb[:, hb * HEAD_DIM:(hb + 1) * HEAD_DIM]

    @pl.when(j == cmp_step)
    def _():
        chunks = zc_ref.shape[1]
        for hb in range(zc_ref.shape[0]):
            col_ref[...] = r[:, hb * HEAD_DIM:(hb + 1) * HEAD_DIM]
            for l in range(CMP_STRIDE):
                rows = col_ref[pl.ds(l, chunks, stride=CMP_STRIDE), :]
                zc_ref[hb, :, l * HEAD_DIM:(l + 1) * HEAD_DIM] = rows.astype(zc_ref.dtype)


def _nsa_proj(x, g, w_all, layer, w_gate, bsz, seq):
    m, k = x.shape
    n_main = N_HEADS * HEAD_DIM + 6 * N_KV * HEAD_DIM
    tm = min(ROW_TILE, seq)
    tn = COL_TILE
    hpb = tn // HEAD_DIM
    spb = seq // tm
    n_cmp = 2 * N_KV
    assert hpb == n_cmp and (N_HEADS * HEAD_DIM) % tn == 0
    cmp_step = N_HEADS * HEAD_DIM // tn
    return pl.pallas_call(
        functools.partial(_nsa_proj_kernel, cmp_step=cmp_step),
        grid=(m // tm, n_main // tn),
        in_specs=[
            pl.BlockSpec((tm, k), lambda i, j: (i, 0)),
            pl.BlockSpec((1, k), lambda i, j: (0, 0)),
            pl.BlockSpec((None, k, tn), lambda i, j: (layer, 0, j)),
            pl.BlockSpec((k, LANES), lambda i, j: (0, 0)),
        ],
        out_specs=[
            pl.BlockSpec((None, hpb, tm, HEAD_DIM), lambda i, j: (i // spb, j, i % spb, 0)),
            pl.BlockSpec((tm, LANES), lambda i, j: (i, 0)),
            pl.BlockSpec(
                (None, n_cmp, tm // CMP_STRIDE, CMP_STRIDE * HEAD_DIM), lambda i, j: (i // spb, 0, i % spb, 0)
            ),
        ],
        out_shape=[
            jax.ShapeDtypeStruct((bsz, n_main // HEAD_DIM, seq, HEAD_DIM), BF16),
            jax.ShapeDtypeStruct((m, LANES), F32),
            jax.ShapeDtypeStruct((bsz, n_cmp, seq // CMP_STRIDE, CMP_STRIDE * HEAD_DIM), BF16),
        ],
        scratch_shapes=[pltpu.VMEM((tm, k), BF16), pltpu.VMEM((tm, HEAD_DIM), F32)],
        compiler_params=_params(
            2, tm * k * 4, k * tn * 2, k * LANES * 2, tm * tn * 2, tm * LANES * 4, tm * tn * 2, tm * k, tm * tn * 4
        ),
        name="nsa_proj",
    )(x, g.reshape(1, k), w_all, w_gate)


def _matmul_residual_kernel(y_ref, w_ref, h_ref, o_ref):
    o_ref[...] = h_ref[...] + _dot(y_ref[...], w_ref[...])


def _matmul_residual(y, w_all, layer, h):
    m, k = y.shape
    n = w_all.shape[2]
    tm = min(WIDE_ROW_TILE, m)
    return pl.pallas_call(
        _matmul_residual_kernel,
        grid=(m // tm,),
        in_specs=[
            pl.BlockSpec((tm, k), lambda i: (i, 0)),
            pl.BlockSpec((None, k, n), lambda i: (layer, 0, 0)),
            pl.BlockSpec((tm, n), lambda i: (i, 0)),
        ],
        out_specs=pl.BlockSpec((tm, n), lambda i: (i, 0)),
        out_shape=jax.ShapeDtypeStruct((m, n), F32),
        compiler_params=_params(1, tm * k * 2, k * n * 2, tm * n * 4, tm * n * 4),
        name="matmul_residual",
    )(y, w_all, h)


def _mlp_kernel(x_ref, g_ref, wu_ref, wd_ref, o_ref, xn_ref, acc_ref):
    j = pl.program_id(1)

    @pl.when(j == 0)
    def _():
        xn_ref[...] = _rms_bf16(x_ref[...], g_ref[...])
        acc_ref[...] = jnp.zeros_like(acc_ref)

    u = _dot(xn_ref[...], wu_ref[...])
    u = jnp.square(jnp.maximum(u, 0.0)).astype(BF16)
    acc_ref[...] += _dot(u, wd_ref[...])

    @pl.when(j == pl.num_programs(1) - 1)
    def _():
        o_ref[...] = x_ref[...] + acc_ref[...]


def _mlp(x, g, w_up, w_down, layer):
    m, d = x.shape
    f = w_up.shape[2]
    tm = min(MLP_ROW_TILE, m)
    tf = min(MLP_FF_TILE, f)
    return pl.pallas_call(
        _mlp_kernel,
        grid=(m // tm, f // tf),
        in_specs=[
            pl.BlockSpec((tm, d), lambda i, j: (i, 0)),
            pl.BlockSpec((1, d), lambda i, j: (0, 0)),
            pl.BlockSpec((None, d, tf), lambda i, j: (layer, 0, j)),
            pl.BlockSpec((None, tf, d), lambda i, j: (layer, j, 0)),
        ],
        out_specs=pl.BlockSpec((tm, d), lambda i, j: (i, 0)),
        out_shape=jax.ShapeDtypeStruct((m, d), F32),
        scratch_shapes=[pltpu.VMEM((tm, d), BF16), pltpu.VMEM((tm, d), F32)],
        compiler_params=_params(2, tm * d * 4, d * tf * 2, tf * d * 2, tm * d * 4, tm * d * 3),
        name="mlp",
    )(x, g.reshape(1, d), w_up, w_down)


def _ple_kernel(x_ref, g_ref, wg_ref, p_ref, wp_ref, gf_ref, o_ref, *, final_norm):
    x = x_ref[...]
    gate = _sigmoid(_dot(_rms_bf16(x, g_ref[...]), wg_ref[...]))
    proj = _dot(p_ref[...].astype(BF16), wp_ref[...])
    h = x + gate * proj
    if final_norm:
        ms = jnp.mean(h * h, axis=-1, keepdims=True)
        h = h * lax.rsqrt(ms + EPS) * gf_ref[...]
    o_ref[...] = h


def _ple(x, g, w_gate, p_all, layer, w_proj, g_final, final_norm):
    m, d = x.shape
    e = p_all.shape[-1]
    tm = min(PLE_ROW_TILE, m)
    const = lambda shape: pl.BlockSpec(shape, lambda i: (0,) * len(shape))
    return pl.pallas_call(
        functools.partial(_ple_kernel, final_norm=final_norm),
        grid=(m // tm,),
        in_specs=[
            pl.BlockSpec((tm, d), lambda i: (i, 0)),
            const((1, d)),
            pl.BlockSpec((None, d, d), lambda i: (layer, 0, 0)),
            pl.BlockSpec((None, tm, e), lambda i: (layer, i, 0)),
            pl.BlockSpec((None, e, d), lambda i: (layer, 0, 0)),
            const((1, d)),
        ],
        out_specs=pl.BlockSpec((tm, d), lambda i: (i, 0)),
        out_shape=jax.ShapeDtypeStruct((m, d), F32),
        compiler_params=_params(1, tm * d * 4, d * d * 2, tm * e * 4, e * d * 2, tm * d * 4, tm * d * 6),
        name="ple",
    )(x, g.reshape(1, d), w_gate, p_all, w_proj, g_final.reshape(1, d))


def _mixer_kernel(
    z_ref, cw_ref, cb_ref, wr_ref, br_ref, wi_ref, bi_ref, lam_ref, pw_ref, ps_ref,
    y_ref, xext_ref, uext_ref, hlast_ref, *, ts, width,
):
    s = pl.program_id(1)
    xh = 8
    uh = 16

    @pl.when(s == 0)
    def _():
        xext_ref[0:xh, :] = jnp.zeros((xh, width), F32)
        uext_ref[0:uh, :] = jnp.zeros((uh, width), F32)
        hlast_ref[...] = jnp.zeros_like(hlast_ref)

    xr = z_ref[:, 0:width]
    gate = z_ref[:, width:2 * width]
    u = z_ref[:, 2 * width:3 * width]

    xext_ref[xh:xh + ts, :] = xr
    conv = cb_ref[...] + cw_ref[CONV_WIDTH - 1:CONV_WIDTH, :] * xr
    for k in range(CONV_WIDTH - 1):
        shift = CONV_WIDTH - 1 - k
        conv = conv + cw_ref[k:k + 1, :] * xext_ref[pl.ds(xh - shift, ts), :]
    xext_ref[0:xh, :] = xr[ts - xh:ts, :]

    gw = wr_ref.shape[1]
    conv_b = conv.astype(BF16)
    r_parts, i_parts = [], []
    for c in range(width // gw):
        xc = conv_b[:, c * gw:(c + 1) * gw]
        r_parts.append(_dot(xc, wr_ref[c]))
        i_parts.append(_dot(xc, wi_ref[c]))
    r = _sigmoid(jnp.concatenate(r_parts, axis=1) + br_ref[...])
    ig = _sigmoid(jnp.concatenate(i_parts, axis=1) + bi_ref[...])

    nl = -lam_ref[...]
    softplus = jnp.maximum(nl, 0.0) + jnp.log(1.0 + jnp.exp(-jnp.abs(nl)))
    log_a = -LRU_C * r * softplus
    a = jnp.exp(log_a)
    b = jnp.sqrt(1.0 - a * a) * (ig * conv)

    sub = 8
    srow = lax.broadcasted_iota(jnp.int32, (sub, 1), 0)
    carry = hlast_ref[...]
    h_groups = []
    for r0 in range(0, ts, sub):
        ag = a[r0:r0 + sub, :]
        bg = b[r0:r0 + sub, :]
        d = 1
        while d < sub:
            keep = srow >= d
            a_sh = jnp.where(keep, pltpu.roll(ag, d, axis=0), 1.0)
            b_sh = jnp.where(keep, pltpu.roll(bg, d, axis=0), 0.0)
            bg = ag * b_sh + bg
            ag = ag * a_sh
            d *= 2
        hg = ag * carry + bg
        carry = hg[sub - 1:sub, :]
        h_groups.append(hg)
    h = jnp.concatenate(h_groups, axis=0)
    hlast_ref[...] = carry
    y_ref[:, 0:width] = (h * _gelu_tanh(gate)).astype(BF16)

    uext_ref[uh:uh + ts, :] = u
    t = s * ts + lax.broadcasted_iota(jnp.int32, (ts, 1), 0)
    gd = width // len(POOL_WINDOWS)
    for gi, win in enumerate(POOL_WINDOWS):
        cs = slice(gi * gd, (gi + 1) * gd)
        wsum = u[:, cs]
        for k in range(1, win):
            wsum = wsum + uext_ref[pl.ds(uh - k, ts), cs]
        cnt = jnp.minimum(t + 1, win).astype(F32)
        dd = wsum / cnt - u[:, cs]
        yp = _dot(dd.astype(BF16), pw_ref[gi]) * ps_ref[:, cs]
        y_ref[:, width + gi * gd:width + (gi + 1) * gd] = yp.astype(BF16)
    uext_ref[0:uh, :] = u[ts - uh:ts, :]


def _mixer(z, conv_w, conv_b, w_r, b_r, w_i, b_i, lam, pool_w, pool_scale):
    bsz, seq, w3 = z.shape
    width = w3 // 3
    ts = min(MIX_TIME_TILE, seq)
    hd = width // LRU_HEADS
    per = 2 * LANES // hd
    gw = per * hd

    def block_diag(w):
        w = w.reshape(LRU_HEADS // per, per, hd, hd)
        eye = jnp.eye(per, dtype=w.dtype)
        return jnp.einsum("cpij,pq->cpiqj", w, eye).reshape(LRU_HEADS // per, gw, gw).astype(BF16)

    row = lambda v: v.reshape(1, width)
    const2 = lambda shape: pl.BlockSpec(shape, lambda b, s: (0,) * len(shape))
    kern = functools.partial(_mixer_kernel, ts=ts, width=width)
    gd = width // len(POOL_WINDOWS)
    return pl.pallas_call(
        kern,
        grid=(bsz, seq // ts),
        in_specs=[
            pl.BlockSpec((None, ts, w3), lambda b, s: (b, s, 0)),
            const2((CONV_WIDTH, width)),
            const2((1, width)),
            const2((LRU_HEADS // per, gw, gw)),
            const2((1, width)),
            const2((LRU_HEADS // per, gw, gw)),
            const2((1, width)),
            const2((1, width)),
            const2((len(POOL_WINDOWS), gd, gd)),
            const2((1, width)),
        ],
        out_specs=pl.BlockSpec((None, ts, 2 * width), lambda b, s: (b, s, 0)),
        out_shape=jax.ShapeDtypeStruct((bsz, seq, 2 * width), BF16),
        scratch_shapes=[
            pltpu.VMEM((ts + 8, width), F32),
            pltpu.VMEM((ts + 16, width), F32),
            pltpu.VMEM((1, width), F32),
        ],
        compiler_params=_params(2, ts * w3 * 4, ts * width * 4, ts * width * 16),
        name="lru_pool_mixer",
    )(
        z, conv_w, row(conv_b), block_diag(w_r), row(b_r), block_diag(w_i), row(b_i), row(lam),
        pool_w.astype(BF16), row(pool_scale),
    )


def _compress_kernel(x_ref, pos_ref, w1_ref, w2_ref, o_ref):
    x = x_ref[...].astype(F32)
    xa = (x + pos_ref[0]).astype(BF16)
    xb = (x + pos_ref[1]).astype(BF16)
    first = _dot(xa, w1_ref[0])
    second = _dot(xb, w1_ref[1])
    n = x.shape[0]
    hid = first + pltpu.roll(second, n - 1, axis=0)
    o_ref[...] = _dot(_gelu_tanh(hid).astype(BF16), w2_ref[...]).astype(o_ref.dtype)


def _compress(zc, pos, w1, w2):
    bsz, _, nc, kk = zc.shape
    two, ng = 2, N_KV
    hid = w1.shape[-1]
    dh = w2.shape[-1]
    return pl.pallas_call(
        _compress_kernel,
        grid=(two, bsz, ng),
        in_specs=[
            pl.BlockSpec((None, None, nc, kk), lambda t, b, g: (b, t * N_KV + g, 0, 0)),
            pl.BlockSpec((None, 2, 1, kk), lambda t, b, g: (t, 0, 0, 0)),
            pl.BlockSpec((None, 2, kk, hid), lambda t, b, g: (t, 0, 0, 0)),
            pl.BlockSpec((None, hid, dh), lambda t, b, g: (t, 0, 0)),
        ],
        out_specs=pl.BlockSpec((None, None, None, nc, dh), lambda t, b, g: (t, b, g, 0, 0)),
        out_shape=jax.ShapeDtypeStruct((two, bsz, ng, nc, dh), BF16),
        compiler_params=_params(3, nc * kk * 2, 2 * kk * hid * 2, hid * dh * 2, nc * dh * 2, nc * kk * 8),
        name="nsa_compress",
    )(zc, pos, w1, w2)


LOG2E = np.float32(1.4426950408889634)
AUG_MASK_ROWS = 64
ALIBI_PARTS = 3


def _t32(x):
    r, c = x.shape
    rows = []
    for cb in range(0, c, LANES):
        rows.append(jnp.concatenate([x[rb:rb + LANES, cb:cb + LANES].T for rb in range(0, r, LANES)], axis=1))
    return jnp.concatenate(rows, axis=0)


def _t_bf16(x):
    return _t32(x.astype(F32)).astype(BF16)


def _nsa_kernel(
    slope_ref, q_ref, kc_ref, vc_ref, ks_ref, vs_ref, kw_ref, vw_ref, gt_ref, ovt_ref, aug_ref, augc_ref,
    o_ref, kp_ref, kwp_ref, kcp_ref, vst_ref, vwt_ref, vct_ref, s_ref, gsc_ref, *, tq, tk, seq,
):
    g = pl.program_id(1)
    i = pl.program_id(2)
    t0 = i * tq
    nl = HPG * tq
    n_sel = seq // SEL_LEN
    ncp = kc_ref.shape[0]

    @pl.when(i == 0)
    def _():
        kp_ref[:, 0:HEAD_DIM] = ks_ref[...]
        kp_ref[:, HEAD_DIM:2 * HEAD_DIM] = aug_ref[...]
        kwp_ref[:, 0:HEAD_DIM] = kw_ref[...]
        kwp_ref[:, HEAD_DIM:2 * HEAD_DIM] = aug_ref[...]
        kcp_ref[:, 0:HEAD_DIM] = kc_ref[...]
        kcp_ref[:, HEAD_DIM:2 * HEAD_DIM] = augc_ref[...]
        for kt in range(seq // tk):
            vst_ref[kt] = _t_bf16(vs_ref[kt * tk:(kt + 1) * tk, :])
        for kb in range(seq // tq):
            vwt_ref[kb] = _t_bf16(vw_ref[kb * tq:(kb + 1) * tq, :])
        vct_ref[...] = _t_bf16(vc_ref[...])

    qscale = np.float32(HEAD_DIM ** -0.5) * LOG2E
    q_t = jnp.concatenate(
        [_t32(q_ref[h].astype(F32) * qscale) for h in range(HPG)], axis=1
    ).astype(BF16)

    lane = lax.broadcasted_iota(jnp.int32, (1, nl), 1)
    slope = jnp.zeros((1, nl), F32)
    for h in range(HPG):
        in_h = (lane >= h * tq) & (lane < (h + 1) * tq)
        slope = jnp.where(in_h, slope_ref[g * HPG + h] * LOG2E, slope)
    t_row = t0 + (lane & (tq - 1))

    arow = lax.broadcasted_iota(jnp.int32, (LANES - AUG_MASK_ROWS, nl), 0)
    alibi_rows = jnp.zeros((LANES - AUG_MASK_ROWS, nl), F32)
    rem = slope
    for k in range(ALIBI_PARTS):
        piece = rem.astype(BF16).astype(F32)
        rem = rem - piece
        alibi_rows = jnp.where(arow == 2 * k, piece * float(SEL_LEN), alibi_rows)
        alibi_rows = jnp.where(arow == 2 * k + 1, piece, alibi_rows)
    qa_plain = jnp.concatenate(
        [q_t, jnp.zeros((AUG_MASK_ROWS, nl), BF16), alibi_rows.astype(BF16)], axis=0
    )

    cend = lax.broadcasted_iota(jnp.int32, (ncp, 1), 0) * CMP_STRIDE + (CMP_LEN - 1)
    valid_c = cend <= t_row
    s_c = jnp.where(valid_c, _dot(kcp_ref[...], qa_plain), NEG)
    m_c = jnp.max(s_c, axis=0, keepdims=True)
    p_c = jnp.where(valid_c, jnp.exp2(s_c - m_c), 0.0)
    l_c = jnp.sum(p_c, axis=0, keepdims=True)
    p_c = p_c * (1.0 / jnp.where(l_c > 0.0, l_c, 1.0))
    o_c = _dot(vct_ref[...], p_c.astype(BF16))

    p_sum = p_c[:, 0:tq]
    for h in range(1, HPG):
        p_sum = p_sum + p_c[:, h * tq:(h + 1) * tq]
    ovt = ovt_ref[...]
    top_n = min(SEL_TOPN, n_sel)
    sub = 8
    mask_cols = []
    for c0 in range(0, tq, LANES):
        imp = jnp.zeros((n_sel, LANES), F32)
        rem = p_sum[:, c0:c0 + LANES]
        for _ in range(3):
            piece = rem.astype(BF16)
            rem = rem - piece.astype(F32)
            imp = imp + _dot(ovt, piece)
        blk = lax.broadcasted_iota(jnp.int32, (n_sel, LANES), 0)
        tl = t0 + c0 + lax.broadcasted_iota(jnp.int32, (n_sel, LANES), 1)
        cur = lax.shift_right_arithmetic(tl, int(np.log2(SEL_LEN)))
        forced = (blk == 0) | (blk == cur) | (blk == cur - 1)
        causal_s = blk * SEL_LEN <= tl
        v = jnp.where(forced, imp + FORCE_BONUS, imp)
        v = jnp.where(causal_s, v, NEG)
        rank_slabs = []
        for r0 in range(0, n_sel, sub):
            vs_ = v[r0:r0 + sub, :]
            lower = blk[r0:r0 + sub, :]
            rk = jnp.zeros((sub, LANES), F32)
            for j in range(n_sel):
                vj = v[j:j + 1, :]
                if j < r0:
                    beats = vj >= vs_
                elif j >= r0 + sub - 1:
                    beats = vj > vs_
                else:
                    beats = (vj > vs_) | ((vj == vs_) & (lower > j))
                rk = rk + jnp.where(beats, 1.0, 0.0)
            rank_slabs.append(rk)
        rank = jnp.concatenate(rank_slabs, axis=0)
        mask_cols.append(jnp.where((rank < top_n) & causal_s, 0.0, NEG))
    mask_rows = mask_cols[0] if len(mask_cols) == 1 else jnp.concatenate(mask_cols, axis=1)
    if n_sel < AUG_MASK_ROWS:
        mask_rows = jnp.concatenate([mask_rows, jnp.zeros((AUG_MASK_ROWS - n_sel, tq), F32)], axis=0)
    mask_rows = jnp.concatenate([mask_rows] * HPG, axis=1).astype(BF16)
    qa_sel = jnp.concatenate([q_t, mask_rows, alibi_rows.astype(BF16)], axis=0)

    span = WINDOW + tq
    start = pl.multiple_of(jnp.maximum(t0 - WINDOW, 0), tq)
    dist = t_row - (start + lax.broadcasted_iota(jnp.int32, (span, 1), 0))
    valid_w = lax.bitcast_convert_type(dist, jnp.uint32) < jnp.uint32(WINDOW)
    s_w = jnp.where(valid_w, _dot(kwp_ref[pl.ds(start, span), :], qa_plain), NEG)
    m_w = jnp.max(s_w, axis=0, keepdims=True)
    p_w = jnp.exp2(s_w - m_w)
    l_w = jnp.sum(p_w, axis=0, keepdims=True)
    p_wb = p_w.astype(BF16)
    sb = start // tq
    acc_w = jnp.zeros((HEAD_DIM, nl), F32)
    for k in range(span // tq):
        acc_w = acc_w + _dot(vwt_ref[sb + k], p_wb[k * tq:(k + 1) * tq, :])

    krow = lax.broadcasted_iota(jnp.int32, (tk, 1), 0)

    def score_tile(kt, masked):
        j0 = pl.multiple_of(kt * tk, tk)
        s = _dot(kp_ref[pl.ds(j0, tk), :], qa_sel)
        if masked:
            s = jnp.where(j0 + krow <= t_row, s, NEG)
        s_ref[...] = s
        return jnp.max(s, axis=0, keepdims=True)

    def finish_tile(kt, m_p, l_p, acc, m_t):
        m_n = jnp.maximum(m_p, m_t)
        p = jnp.exp2(s_ref[...] - m_n)
        alpha = jnp.exp2(m_p - m_n)
        l_n = alpha * l_p + jnp.sum(p, axis=0, keepdims=True)
        acc = alpha * acc + _dot(vst_ref[kt], p.astype(BF16))
        return m_n, l_n, acc

    n_kt = (t0 + tq - 1) // tk + 1

    def pipe_step(kt, carry):
        m_p, l_p, acc, m_t, prev = carry
        m_n, l_n, acc = finish_tile(prev, m_p, l_p, acc, m_t)
        return m_n, l_n, acc, score_tile(kt, False), kt

    carry = (
        jnp.full((1, nl), NEG, F32), jnp.zeros((1, nl), F32), jnp.zeros((HEAD_DIM, nl), F32),
        score_tile(n_kt - 1, True), n_kt - 1,
    )
    m_p, l_p, acc, m_t, prev = lax.fori_loop(0, n_kt - 1, pipe_step, carry)
    _, l_s, acc_s = finish_tile(prev, m_p, l_p, acc, m_t)

    gsc_ref[...] = _t32(_sigmoid(gt_ref[...]))

    def gate_row(br):
        first = br * N_HEADS + g * HPG
        return jnp.concatenate([gsc_ref[pl.ds(first + h, 1), :] for h in range(HPG)], axis=1)

    o_t = (
        gate_row(0) * o_c
        + (gate_row(1) * (1.0 / l_s)) * acc_s
        + (gate_row(2) * (1.0 / l_w)) * acc_w
    )
    for h in range(HPG):
        o_ref[:, h * HEAD_DIM:(h + 1) * HEAD_DIM] = _t32(o_t[:, h * tq:(h + 1) * tq]).astype(o_ref.dtype)


def _nsa_attention(z, kv_cmp, gates):
    bsz, _, seq, _ = z.shape
    tq = min(ATT_Q_TILE, seq)
    tk = min(ATT_K_TILE, seq)
    n_sel = seq // SEL_LEN
    ncp = kv_cmp.shape[3]
    qw = HPG * HEAD_DIM
    kv0 = N_HEADS
    assert n_sel <= AUG_MASK_ROWS and tq & (tq - 1) == 0 and seq % tk == 0 and tk % tq == 0

    cs = np.arange(ncp)[None, :] * CMP_STRIDE
    ss = np.arange(n_sel)[:, None] * SEL_LEN
    ovt = ((cs < ss + SEL_LEN) & (cs + CMP_LEN > ss) & (np.arange(ncp)[None, :] < ncp - 1)).astype(np.float32)

    def aug_cols(pos):
        a = np.zeros((pos.shape[0], LANES), np.float32)
        a[np.arange(pos.shape[0]), np.minimum(pos // SEL_LEN, AUG_MASK_ROWS - 1)] = 1.0
        for k in range(ALIBI_PARTS):
            a[:, AUG_MASK_ROWS + 2 * k] = pos // SEL_LEN
            a[:, AUG_MASK_ROWS + 2 * k + 1] = pos % SEL_LEN
        return jnp.asarray(a, BF16)

    aug = aug_cols(np.arange(seq))
    augc = aug_cols(np.arange(ncp) * CMP_STRIDE + CMP_LEN - 1)
    slopes = jnp.exp2(-8.0 * jnp.arange(1, N_HEADS + 1, dtype=F32) / N_HEADS)

    def kv_spec(col):
        return pl.BlockSpec((None, None, seq, HEAD_DIM), lambda b, g, i, col=col: (b, kv0 + col * N_KV + g, 0, 0))

    const = lambda shape: pl.BlockSpec(shape, lambda b, g, i: (0,) * len(shape))
    kern = functools.partial(_nsa_kernel, tq=tq, tk=tk, seq=seq)
    return pl.pallas_call(
        kern,
        grid=(bsz, N_KV, seq // tq),
        in_specs=[
            pl.BlockSpec(memory_space=pltpu.SMEM),
            pl.BlockSpec((None, HPG, tq, HEAD_DIM), lambda b, g, i: (b, g, i, 0)),
            pl.BlockSpec((None, None, None, ncp, HEAD_DIM), lambda b, g, i: (0, b, g, 0, 0)),
            pl.BlockSpec((None, None, None, ncp, HEAD_DIM), lambda b, g, i: (1, b, g, 0, 0)),
            kv_spec(2), kv_spec(3), kv_spec(4), kv_spec(5),
            pl.BlockSpec((None, tq, LANES), lambda b, g, i: (b, i, 0)),
            const((n_sel, ncp)),
            const((seq, LANES)),
            const((ncp, LANES)),
        ],
        out_specs=pl.BlockSpec((None, tq, qw), lambda b, g, i: (b, i, g)),
        out_shape=jax.ShapeDtypeStruct((bsz, seq, N_HEADS * HEAD_DIM), BF16),
        scratch_shapes=[
            pltpu.VMEM((seq, 2 * HEAD_DIM), BF16),
            pltpu.VMEM((seq, 2 * HEAD_DIM), BF16),
            pltpu.VMEM((ncp, 2 * HEAD_DIM), BF16),
            pltpu.VMEM((seq // tk, HEAD_DIM, tk), BF16),
            pltpu.VMEM((seq // tq, HEAD_DIM, tq), BF16),
            pltpu.VMEM((HEAD_DIM, ncp), BF16),
            pltpu.VMEM((tk, HPG * tq), F32),
            pltpu.VMEM((LANES, tq), F32),
        ],
        compiler_params=_params(3, 5 * seq * HEAD_DIM * 2, 3 * seq * HEAD_DIM * 2, HPG * tq * tk * 12),
        name="nsa_attention",
    )(slopes, z, kv_cmp, kv_cmp, z, z, z, z, gates, jnp.asarray(ovt, BF16), aug, augc)


def _ab_layer(h, bsz, seq, j, ln_g, w_in, conv_w, conv_b, w_r, b_r, w_i, b_i, lam, pool_w, pool_scale, w_out):
    z = _norm_matmul(h, ln_g, w_in, j, F32)
    y = _mixer(z.reshape(bsz, seq, -1), conv_w, conv_b, w_r, b_r, w_i, b_i, lam, pool_w, pool_scale)
    return _matmul_residual(y.reshape(bsz * seq, -1), w_out, j, h)


def _nsa_layer(h, bsz, seq, j, ln_g, w_in, w_gate, pos_k, w1_k, w2_k, pos_v, w1_v, w2_v, w_out):
    z, gl, zc = _nsa_proj(h, ln_g, w_in, j, w_gate, bsz, seq)
    half = CMP_LEN // CMP_STRIDE
    pos = jnp.stack([pos_k, pos_v]).reshape(2, half, 1, CMP_STRIDE * HEAD_DIM)
    w1 = jnp.stack([w1_k, w1_v]).reshape(2, half, CMP_STRIDE * HEAD_DIM, -1).astype(BF16)
    w2 = jnp.stack([w2_k, w2_v]).astype(BF16)
    kv_cmp = _compress(zc, pos, w1, w2)

    o = _nsa_attention(z, kv_cmp, gl.reshape(bsz, seq, LANES))
    return _matmul_residual(o.reshape(bsz * seq, -1), w_out, j, h)


def kernel(x, p, ln_mix_g, ab_w_in, ab_conv_w, ab_conv_b, ab_w_rgate, ab_b_rgate, ab_w_igate, ab_b_igate, ab_lambda, ab_pool_w, ab_pool_scale, ab_w_out, c_w_in, c_cmp_pos_k, c_cmp_w1_k, c_cmp_w2_k, c_cmp_pos_v, c_cmp_w1_v, c_cmp_w2_v, c_w_out, ln_mlp_g, mlp_w_up, mlp_w_down, ln_ple_g, ple_w_gate, ple_w_proj, ln_final_g):
    bsz, seq, d = x.shape
    depth = p.shape[0]
    h = x.reshape(bsz * seq, d)
    p_all = p.reshape(depth, bsz * seq, -1)
    ab_in, ab_out = ab_w_in.astype(BF16), ab_w_out.astype(BF16)
    c_in, c_out = c_w_in.astype(BF16), c_w_out.astype(BF16)
    w_up, w_down = mlp_w_up.astype(BF16), mlp_w_down.astype(BF16)
    w_pg, w_pp = ple_w_gate.astype(BF16), ple_w_proj.astype(BF16)
    n_main = N_HEADS * HEAD_DIM + 6 * N_KV * HEAD_DIM
    n_gate = N_BRANCH * N_HEADS
    for i in range(depth):
        j = i // 2
        if i % 2 == 0:
            h = _ab_layer(
                h, bsz, seq, j, ln_mix_g[i], ab_in, ab_conv_w[j], ab_conv_b[j], ab_w_rgate[j], ab_b_rgate[j],
                ab_w_igate[j], ab_b_igate[j], ab_lambda[j], ab_pool_w[j], ab_pool_scale[j], ab_out,
            )
        else:
            w_gate = jnp.pad(c_w_in[j, :, n_main:], ((0, 0), (0, LANES - n_gate))).astype(BF16)
            h = _nsa_layer(
                h, bsz, seq, j, ln_mix_g[i], c_in, w_gate, c_cmp_pos_k[j], c_cmp_w1_k[j], c_cmp_w2_k[j],
                c_cmp_pos_v[j], c_cmp_w1_v[j], c_cmp_w2_v[j], c_out,
            )
        h = _mlp(h, ln_mlp_g[i], w_up, w_down, i)
        h = _ple(h, ln_ple_g[i], w_pg, p_all, i, w_pp, ln_final_g, final_norm=(i == depth - 1))
    return h.reshape(bsz, seq, d)
```

```python
import functools

import jax
import jax.numpy as jnp
import numpy as np
from jax import lax
from jax.experimental import pallas as pl
from jax.experimental.pallas import tpu as pltpu

F32 = jnp.float32
BF16 = jnp.bfloat16

EPS = 1e-6
NEG = -1e30
FORCE_BONUS = 1e6
LRU_C = 8.0

LANES = 128
V7X_SCOPED_VMEM_CAP = 60000 * 1024
COMPILER_TEMP_BYTES = 12 * 1024 * 1024

LRU_HEADS = 16
POOL_WINDOWS = (2, 4, 8, 16)
CONV_WIDTH = 4
N_HEADS = 16
N_KV = 4
HPG = N_HEADS // N_KV
HEAD_DIM = 128
CMP_LEN = 32
CMP_STRIDE = 16
SEL_LEN = 64
SEL_TOPN = 16
WINDOW = 512
N_BRANCH = 3

ROW_TILE = 1024
COL_TILE = 1024
MLP_ROW_TILE = 512
MLP_FF_TILE = 1024
PLE_ROW_TILE = 512
WIDE_ROW_TILE = 512
MIX_TIME_TILE = 256
ATT_Q_TILE = 256
ATT_K_TILE = 512


def _params(n_axes, *block_bytes):
    need = 2 * sum(block_bytes) + COMPILER_TEMP_BYTES
    return pltpu.CompilerParams(
        dimension_semantics=("arbitrary",) * n_axes,
        vmem_limit_bytes=int(min(need, V7X_SCOPED_VMEM_CAP)),
    )


def _rms_bf16(x, g):
    ms = jnp.mean(x * x, axis=-1, keepdims=True)
    return (x * lax.rsqrt(ms + EPS) * g).astype(BF16)


def _sigmoid(x):
    return 1.0 / (1.0 + jnp.exp(-x))


def _gelu_tanh(x):
    c = np.float32(np.sqrt(2.0 / np.pi))
    return x * (0.5 * (1.0 + jnp.tanh(c * (x + 0.044715 * (x * x * x)))))


def _dot(a, b):
    return jnp.dot(a, b, preferred_element_type=F32)


def _dot_nt(a, b):
    return lax.dot_general(a, b, (((1,), (1,)), ((), ())), preferred_element_type=F32)


def _norm_matmul_kernel(x_ref, g_ref, w_ref, o_ref):
    o_ref[...] = _dot(_rms_bf16(x_ref[...], g_ref[...]), w_ref[...]).astype(o_ref.dtype)


def _norm_matmul(x, g, w_all, layer, out_dtype):
    m, k = x.shape
    n = w_all.shape[2]
    tm = min(WIDE_ROW_TILE, m)
    osz = jnp.dtype(out_dtype).itemsize
    return pl.pallas_call(
        _norm_matmul_kernel,
        grid=(m // tm,),
        in_specs=[
            pl.BlockSpec((tm, k), lambda i: (i, 0)),
            pl.BlockSpec((1, k), lambda i: (0, 0)),
            pl.BlockSpec((None, k, n), lambda i: (layer, 0, 0)),
        ],
        out_specs=pl.BlockSpec((tm, n), lambda i: (i, 0)),
        out_shape=jax.ShapeDtypeStruct((m, n), out_dtype),
        compiler_params=_params(1, tm * k * 4, k * n * 2, tm * n * osz),
        name="norm_matmul",
    )(x, g.reshape(1, k), w_all)


def _nsa_proj_kernel(xn_ref, w_ref, wg_ref, z_ref, gl_ref, zc_ref, col_ref, *, cmp_step):
    j = pl.program_id(1)

    @pl.when(j == 0)
    def _():
        gl_ref[...] = _dot(xn_ref[...], wg_ref[...])

    r = _dot(xn_ref[...], w_ref[...])
    rb = r.astype(z_ref.dtype)
    for hb in range(z_ref.shape[0]):
        z_ref[hb] = rb[:, hb * HEAD_DIM:(hb + 1) * HEAD_DIM]

    @pl.when(j == cmp_step)
    def _():
        chunks = zc_ref.shape[1]
        for hb in range(zc_ref.shape[0]):
            col_ref[...] = r[:, hb * HEAD_DIM:(hb + 1) * HEAD_DIM]
            for l in range(CMP_STRIDE):
                rows = col_ref[pl.ds(l, chunks, stride=CMP_STRIDE), :]
                zc_ref[hb, :, l * HEAD_DIM:(l + 1) * HEAD_DIM] = rows.astype(zc_ref.dtype)


def _nsa_proj(xn, w_all, layer, w_gate, bsz, seq):
    m, k = xn.shape
    n_main = w_all.shape[2]
    tm = min(ROW_TILE, seq)
    tn = COL_TILE
    hpb = tn // HEAD_DIM
    spb = seq // tm
    n_cmp = 2 * N_KV
    assert hpb == n_cmp and (N_HEADS * HEAD_DIM) % tn == 0
    cmp_step = N_HEADS * HEAD_DIM // tn
    return pl.pallas_call(
        functools.partial(_nsa_proj_kernel, cmp_step=cmp_step),
        grid=(m // tm, n_main // tn),
        in_specs=[
            pl.BlockSpec((tm, k), lambda i, j: (i, 0)),
            pl.BlockSpec((None, k, tn), lambda i, j: (layer, 0, j)),
            pl.BlockSpec((k, LANES), lambda i, j: (0, 0)),
        ],
        out_specs=[
            pl.BlockSpec((None, hpb, tm, HEAD_DIM), lambda i, j: (i // spb, j, i % spb, 0)),
            pl.BlockSpec((tm, LANES), lambda i, j: (i, 0)),
            pl.BlockSpec(
                (None, n_cmp, tm // CMP_STRIDE, CMP_STRIDE * HEAD_DIM), lambda i, j: (i // spb, 0, i % spb, 0)
            ),
        ],
        out_shape=[
            jax.ShapeDtypeStruct((bsz, n_main // HEAD_DIM, seq, HEAD_DIM), BF16),
            jax.ShapeDtypeStruct((m, LANES), F32),
            jax.ShapeDtypeStruct((bsz, n_cmp, seq // CMP_STRIDE, CMP_STRIDE * HEAD_DIM), BF16),
        ],
        scratch_shapes=[pltpu.VMEM((tm, HEAD_DIM), F32)],
        compiler_params=_params(
            2, tm * k * 2, k * tn * 2, k * LANES * 2, tm * tn * 2, tm * LANES * 4, tm * tn * 2, tm * tn * 4
        ),
        name="nsa_proj",
    )(xn, w_all, w_gate)


def _matmul_residual_kernel(y_ref, w_ref, h_ref, gn_ref, o_ref, on_ref):
    h = h_ref[...] + _dot(y_ref[...], w_ref[...])
    o_ref[...] = h
    on_ref[...] = _rms_bf16(h, gn_ref[...])


def _matmul_residual(y, w_all, layer, h, g_next):
    m, k = y.shape
    n = w_all.shape[2]
    tm = min(WIDE_ROW_TILE, m)
    return pl.pallas_call(
        _matmul_residual_kernel,
        grid=(m // tm,),
        in_specs=[
            pl.BlockSpec((tm, k), lambda i: (i, 0)),
            pl.BlockSpec((None, k, n), lambda i: (layer, 0, 0)),
            pl.BlockSpec((tm, n), lambda i: (i, 0)),
            pl.BlockSpec((1, n), lambda i: (0, 0)),
        ],
        out_specs=[pl.BlockSpec((tm, n), lambda i: (i, 0)), pl.BlockSpec((tm, n), lambda i: (i, 0))],
        out_shape=[jax.ShapeDtypeStruct((m, n), F32), jax.ShapeDtypeStruct((m, n), BF16)],
        compiler_params=_params(1, tm * k * 2, k * n * 2, tm * n * 4, tm * n * 4, tm * n * 2),
        name="matmul_residual",
    )(y, w_all, h, g_next.reshape(1, n))


def _mlp_kernel(x_ref, xn_ref, wu_ref, wd_ref, o_ref, acc_ref):
    j = pl.program_id(1)

    @pl.when(j == 0)
    def _():
        acc_ref[...] = jnp.zeros_like(acc_ref)

    u = _dot(xn_ref[...], wu_ref[...])
    u = jnp.square(jnp.maximum(u, 0.0)).astype(BF16)
    acc_ref[...] += _dot(u, wd_ref[...])

    @pl.when(j == pl.num_programs(1) - 1)
    def _():
        o_ref[...] = x_ref[...] + acc_ref[...]


def _mlp(x, xn, w_up, w_down, layer):
    m, d = x.shape
    f = w_up.shape[2]
    tm = min(MLP_ROW_TILE, m)
    tf = min(MLP_FF_TILE, f)
    return pl.pallas_call(
        _mlp_kernel,
        grid=(m // tm, f // tf),
        in_specs=[
            pl.BlockSpec((tm, d), lambda i, j: (i, 0)),
            pl.BlockSpec((tm, d), lambda i, j: (i, 0)),
            pl.BlockSpec((None, d, tf), lambda i, j: (layer, 0, j)),
            pl.BlockSpec((None, tf, d), lambda i, j: (layer, j, 0)),
        ],
        out_specs=pl.BlockSpec((tm, d), lambda i, j: (i, 0)),
        out_shape=jax.ShapeDtypeStruct((m, d), F32),
        scratch_shapes=[pltpu.VMEM((tm, d), F32)],
        compiler_params=_params(2, tm * d * 4, tm * d * 2, d * tf * 2, tf * d * 2, tm * d * 4, tm * d * 2),
        name="mlp",
    )(x, xn, w_up, w_down)


def _ple_kernel(x_ref, g_ref, wg_ref, p_ref, wp_ref, gn_ref, o_ref, *, mode):
    x = x_ref[...]
    gate = _sigmoid(_dot(_rms_bf16(x, g_ref[...]), wg_ref[...]))
    proj = _dot(p_ref[...].astype(BF16), wp_ref[...])
    h = x + gate * proj
    if mode == "final":
        ms = jnp.mean(h * h, axis=-1, keepdims=True)
        o_ref[0][...] = h * lax.rsqrt(ms + EPS) * gn_ref[...]
    else:
        o_ref[0][...] = h
        if mode == "next":
            o_ref[1][...] = _rms_bf16(h, gn_ref[...])


def _ple(x, g, w_gate, p_all, layer, w_proj, g_next, mode):
    m, d = x.shape
    e = p_all.shape[-1]
    tm = min(PLE_ROW_TILE, m)
    const = lambda shape: pl.BlockSpec(shape, lambda i: (0,) * len(shape))
    row_spec = pl.BlockSpec((tm, d), lambda i: (i, 0))
    out_specs, out_shape = [row_spec], [jax.ShapeDtypeStruct((m, d), F32)]
    if mode == "next":
        out_specs.append(row_spec)
        out_shape.append(jax.ShapeDtypeStruct((m, d), BF16))

    def body(*refs):
        _ple_kernel(*refs[:6], refs[6:], mode=mode)

    return pl.pallas_call(
        body,
        grid=(m // tm,),
        in_specs=[
            row_spec,
            const((1, d)),
            pl.BlockSpec((None, d, d), lambda i: (layer, 0, 0)),
            pl.BlockSpec((None, tm, e), lambda i: (layer, i, 0)),
            pl.BlockSpec((None, e, d), lambda i: (layer, 0, 0)),
            const((1, d)),
        ],
        out_specs=out_specs,
        out_shape=out_shape,
        compiler_params=_params(1, tm * d * 4, d * d * 2, tm * e * 4, e * d * 2, tm * d * 4, tm * d * 8),
        name="ple",
    )(x, g.reshape(1, d), w_gate, p_all, w_proj, g_next.reshape(1, d))


def _mixer_kernel(
    z_ref, cw_ref, cb_ref, wr_ref, br_ref, wi_ref, bi_ref, lam_ref, pw_ref, ps_ref,
    y_ref, xext_ref, uext_ref, hlast_ref, *, ts, width,
):
    s = pl.program_id(1)
    xh = 8
    uh = 16

    @pl.when(s == 0)
    def _():
        xext_ref[0:xh, :] = jnp.zeros((xh, width), F32)
        uext_ref[0:uh, :] = jnp.zeros((uh, width), F32)
        hlast_ref[...] = jnp.zeros_like(hlast_ref)

    xr = z_ref[:, 0:width]
    gate = z_ref[:, width:2 * width]
    u = z_ref[:, 2 * width:3 * width]

    xext_ref[xh:xh + ts, :] = xr
    conv = cb_ref[...] + cw_ref[CONV_WIDTH - 1:CONV_WIDTH, :] * xr
    for k in range(CONV_WIDTH - 1):
        shift = CONV_WIDTH - 1 - k
        conv = conv + cw_ref[k:k + 1, :] * xext_ref[pl.ds(xh - shift, ts), :]
    xext_ref[0:xh, :] = xr[ts - xh:ts, :]

    gw = wr_ref.shape[1]
    conv_b = conv.astype(BF16)
    r_parts, i_parts = [], []
    for c in range(width // gw):
        xc = conv_b[:, c * gw:(c + 1) * gw]
        r_parts.append(_dot(xc, wr_ref[c]))
        i_parts.append(_dot(xc, wi_ref[c]))
    r = _sigmoid(jnp.concatenate(r_parts, axis=1) + br_ref[...])
    ig = _sigmoid(jnp.concatenate(i_parts, axis=1) + bi_ref[...])

    nl = -lam_ref[...]
    softplus = jnp.maximum(nl, 0.0) + jnp.log(1.0 + jnp.exp(-jnp.abs(nl)))
    log_a = -LRU_C * r * softplus
    a = jnp.exp(log_a)
    b = jnp.sqrt(1.0 - a * a) * (ig * conv)

    sub = 8
    srow = lax.broadcasted_iota(jnp.int32, (sub, 1), 0)
    carry = hlast_ref[...]
    h_groups = []
    for r0 in range(0, ts, sub):
        ag = a[r0:r0 + sub, :]
        bg = b[r0:r0 + sub, :]
        d = 1
        while d < sub:
            keep = srow >= d
            a_sh = jnp.where(keep, pltpu.roll(ag, d, axis=0), 1.0)
            b_sh = jnp.where(keep, pltpu.roll(bg, d, axis=0), 0.0)
            bg = ag * b_sh + bg
            ag = ag * a_sh
            d *= 2
        hg = ag * carry + bg
        carry = hg[sub - 1:sub, :]
        h_groups.append(hg)
    h = jnp.concatenate(h_groups, axis=0)
    hlast_ref[...] = carry
    y_ref[:, 0:width] = (h * _gelu_tanh(gate)).astype(BF16)

    uext_ref[uh:uh + ts, :] = u
    t = s * ts + lax.broadcasted_iota(jnp.int32, (ts, 1), 0)
    gd = width // len(POOL_WINDOWS)
    for gi, win in enumerate(POOL_WINDOWS):
        cs = slice(gi * gd, (gi + 1) * gd)
        wsum = u[:, cs]
        for k in range(1, win):
            wsum = wsum + uext_ref[pl.ds(uh - k, ts), cs]
        cnt = jnp.minimum(t + 1, win).astype(F32)
        dd = wsum / cnt - u[:, cs]
        yp = _dot(dd.astype(BF16), pw_ref[gi]) * ps_ref[:, cs]
        y_ref[:, width + gi * gd:width + (gi + 1) * gd] = yp.astype(BF16)
    uext_ref[0:uh, :] = u[ts - uh:ts, :]


def _mixer(z, conv_w, conv_b, w_r, b_r, w_i, b_i, lam, pool_w, pool_scale):
    bsz, seq, w3 = z.shape
    width = w3 // 3
    ts = min(MIX_TIME_TILE, seq)
    hd = width // LRU_HEADS
    per = 2 * LANES // hd
    gw = per * hd

    def block_diag(w):
        w = w.reshape(LRU_HEADS // per, per, hd, hd)
        eye = jnp.eye(per, dtype=w.dtype)
        return jnp.einsum("cpij,pq->cpiqj", w, eye).reshape(LRU_HEADS // per, gw, gw).astype(BF16)

    row = lambda v: v.reshape(1, width)
    const2 = lambda shape: pl.BlockSpec(shape, lambda b, s: (0,) * len(shape))
    kern = functools.partial(_mixer_kernel, ts=ts, width=width)
    gd = width // len(POOL_WINDOWS)
    return pl.pallas_call(
        kern,
        grid=(bsz, seq // ts),
        in_specs=[
            pl.BlockSpec((None, ts, w3), lambda b, s: (b, s, 0)),
            const2((CONV_WIDTH, width)),
            const2((1, width)),
            const2((LRU_HEADS // per, gw, gw)),
            const2((1, width)),
            const2((LRU_HEADS // per, gw, gw)),
            const2((1, width)),
            const2((1, width)),
            const2((len(POOL_WINDOWS), gd, gd)),
            const2((1, width)),
        ],
        out_specs=pl.BlockSpec((None, ts, 2 * width), lambda b, s: (b, s, 0)),
        out_shape=jax.ShapeDtypeStruct((bsz, seq, 2 * width), BF16),
        scratch_shapes=[
            pltpu.VMEM((ts + 8, width), F32),
            pltpu.VMEM((ts + 16, width), F32),
            pltpu.VMEM((1, width), F32),
        ],
        compiler_params=_params(2, ts * w3 * 4, ts * width * 4, ts * width * 16),
        name="lru_pool_mixer",
    )(
        z, conv_w, row(conv_b), block_diag(w_r), row(b_r), block_diag(w_i), row(b_i), row(lam),
        pool_w.astype(BF16), row(pool_scale),
    )


def _compress_kernel(x_ref, pos_ref, w1_ref, w2_ref, o_ref):
    x = x_ref[...].astype(F32)
    xa = (x + pos_ref[0]).astype(BF16)
    xb = (x + pos_ref[1]).astype(BF16)
    first = _dot(xa, w1_ref[0])
    second = _dot(xb, w1_ref[1])
    n = x.shape[0]
    hid = first + pltpu.roll(second, n - 1, axis=0)
    o_ref[...] = _dot(_gelu_tanh(hid).astype(BF16), w2_ref[...]).astype(o_ref.dtype)


def _compress(zc, pos, w1, w2):
    bsz, _, nc, kk = zc.shape
    two, ng = 2, N_KV
    hid = w1.shape[-1]
    dh = w2.shape[-1]
    return pl.pallas_call(
        _compress_kernel,
        grid=(two, bsz, ng),
        in_specs=[
            pl.BlockSpec((None, None, nc, kk), lambda t, b, g: (b, t * N_KV + g, 0, 0)),
            pl.BlockSpec((None, 2, 1, kk), lambda t, b, g: (t, 0, 0, 0)),
            pl.BlockSpec((None, 2, kk, hid), lambda t, b, g: (t, 0, 0, 0)),
            pl.BlockSpec((None, hid, dh), lambda t, b, g: (t, 0, 0)),
        ],
        out_specs=pl.BlockSpec((None, None, None, nc, dh), lambda t, b, g: (t, b, g, 0, 0)),
        out_shape=jax.ShapeDtypeStruct((two, bsz, ng, nc, dh), BF16),
        compiler_params=_params(3, nc * kk * 2, 2 * kk * hid * 2, hid * dh * 2, nc * dh * 2, nc * kk * 8),
        name="nsa_compress",
    )(zc, pos, w1, w2)


LOG2E = np.float32(1.4426950408889634)
AUG_MASK_ROWS = 64
ALIBI_PARTS = 3


def _t32(x):
    r, c = x.shape
    rows = []
    for cb in range(0, c, LANES):
        rows.append(jnp.concatenate([x[rb:rb + LANES, cb:cb + LANES].T for rb in range(0, r, LANES)], axis=1))
    return jnp.concatenate(rows, axis=0)


def _t_bf16(x):
    return _t32(x.astype(F32)).astype(BF16)


def _nsa_kernel(
    slope_ref, q_ref, kc_ref, vc_ref, ks_ref, vs_ref, kw_ref, vw_ref, gt_ref, ovt_ref, aug_ref, augc_ref,
    o_ref, kp_ref, kwp_ref, kcp_ref, vst_ref, vwt_ref, vct_ref, s_ref, gsc_ref, *, tq, tk, seq,
):
    g = pl.program_id(1)
    i = pl.program_id(2)
    t0 = i * tq
    nl = HPG * tq
    n_sel = seq // SEL_LEN
    ncp = kc_ref.shape[0]

    @pl.when(i == 0)
    def _():
        kp_ref[:, 0:HEAD_DIM] = ks_ref[...]
        kp_ref[:, HEAD_DIM:2 * HEAD_DIM] = aug_ref[...]
        kwp_ref[:, 0:HEAD_DIM] = kw_ref[...]
        kwp_ref[:, HEAD_DIM:2 * HEAD_DIM] = aug_ref[...]
        kcp_ref[:, 0:HEAD_DIM] = kc_ref[...]
        kcp_ref[:, HEAD_DIM:2 * HEAD_DIM] = augc_ref[...]
        for kt in range(seq // tk):
            vst_ref[kt] = _t_bf16(vs_ref[kt * tk:(kt + 1) * tk, :])
        for kb in range(seq // tq):
            vwt_ref[kb] = _t_bf16(vw_ref[kb * tq:(kb + 1) * tq, :])
        vct_ref[...] = _t_bf16(vc_ref[...])

    qscale = np.float32(HEAD_DIM ** -0.5) * LOG2E
    q_t = jnp.concatenate(
        [_t32(q_ref[h].astype(F32) * qscale) for h in range(HPG)], axis=1
    ).astype(BF16)

    lane = lax.broadcasted_iota(jnp.int32, (1, nl), 1)
    slope = jnp.zeros((1, nl), F32)
    for h in range(HPG):
        in_h = (lane >= h * tq) & (lane < (h + 1) * tq)
        slope = jnp.where(in_h, slope_ref[g * HPG + h] * LOG2E, slope)
    t_row = t0 + (lane & (tq - 1))

    arow = lax.broadcasted_iota(jnp.int32, (LANES - AUG_MASK_ROWS, nl), 0)
    alibi_rows = jnp.zeros((LANES - AUG_MASK_ROWS, nl), F32)
    rem = slope
    for k in range(ALIBI_PARTS):
        piece = rem.astype(BF16).astype(F32)
        rem = rem - piece
        alibi_rows = jnp.where(arow == 2 * k, piece * float(SEL_LEN), alibi_rows)
        alibi_rows = jnp.where(arow == 2 * k + 1, piece, alibi_rows)
    qa_plain = jnp.concatenate(
        [q_t, jnp.zeros((AUG_MASK_ROWS, nl), BF16), alibi_rows.astype(BF16)], axis=0
    )

    cend = lax.broadcasted_iota(jnp.int32, (ncp, 1), 0) * CMP_STRIDE + (CMP_LEN - 1)
    valid_c = cend <= t_row
    s_c = jnp.where(valid_c, _dot(kcp_ref[...], qa_plain), NEG)
    m_c = jnp.max(s_c, axis=0, keepdims=True)
    p_c = jnp.where(valid_c, jnp.exp2(s_c - m_c), 0.0)
    l_c = jnp.sum(p_c, axis=0, keepdims=True)
    p_c = p_c * (1.0 / jnp.where(l_c > 0.0, l_c, 1.0))
    o_c = _dot(vct_ref[...], p_c.astype(BF16))

    p_sum = p_c[:, 0:tq]
    for h in range(1, HPG):
        p_sum = p_sum + p_c[:, h * tq:(h + 1) * tq]
    ovt = ovt_ref[...]
    top_n = min(SEL_TOPN, n_sel)
    sub = 8
    mask_cols = []
    for c0 in range(0, tq, LANES):
        imp = jnp.zeros((n_sel, LANES), F32)
        rem = p_sum[:, c0:c0 + LANES]
        for _ in range(3):
            piece = rem.astype(BF16)
            rem = rem - piece.astype(F32)
            imp = imp + _dot(ovt, piece)
        blk = lax.broadcasted_iota(jnp.int32, (n_sel, LANES), 0)
        tl = t0 + c0 + lax.broadcasted_iota(jnp.int32, (n_sel, LANES), 1)
        cur = lax.shift_right_arithmetic(tl, int(np.log2(SEL_LEN)))
        forced = (blk == 0) | (blk == cur) | (blk == cur - 1)
        causal_s = blk * SEL_LEN <= tl
        v = jnp.where(forced, imp + FORCE_BONUS, imp)
        v = jnp.where(causal_s, v, NEG)
        rank_slabs = []
        for r0 in range(0, n_sel, sub):
            vs_ = v[r0:r0 + sub, :]
            lower = blk[r0:r0 + sub, :]
            rk = jnp.zeros((sub, LANES), F32)
            for j in range(n_sel):
                vj = v[j:j + 1, :]
                if j < r0:
                    beats = vj >= vs_
                elif j >= r0 + sub - 1:
                    beats = vj > vs_
                else:
                    beats = (vj > vs_) | ((vj == vs_) & (lower > j))
                rk = rk + jnp.where(beats, 1.0, 0.0)
            rank_slabs.append(rk)
        rank = jnp.concatenate(rank_slabs, axis=0)
        mask_cols.append(jnp.where((rank < top_n) & causal_s, 0.0, NEG))
    mask_rows = mask_cols[0] if len(mask_cols) == 1 else jnp.concatenate(mask_cols, axis=1)
    if n_sel < AUG_MASK_ROWS:
        mask_rows = jnp.concatenate([mask_rows, jnp.zeros((AUG_MASK_ROWS - n_sel, tq), F32)], axis=0)
    mask_rows = jnp.concatenate([mask_rows] * HPG, axis=1).astype(BF16)
    qa_sel = jnp.concatenate([q_t, mask_rows, alibi_rows.astype(BF16)], axis=0)

    span = WINDOW + tq
    start = pl.multiple_of(jnp.maximum(t0 - WINDOW, 0), tq)
    dist = t_row - (start + lax.broadcasted_iota(jnp.int32, (span, 1), 0))
    valid_w = lax.bitcast_convert_type(dist, jnp.uint32) < jnp.uint32(WINDOW)
    s_w = jnp.where(valid_w, _dot(kwp_ref[pl.ds(start, span), :], qa_plain), NEG)
    m_w = jnp.max(s_w, axis=0, keepdims=True)
    p_w = jnp.exp2(s_w - m_w)
    l_w = jnp.sum(p_w, axis=0, keepdims=True)
    p_wb = p_w.astype(BF16)
    sb = start // tq
    acc_w = jnp.zeros((HEAD_DIM, nl), F32)
    for k in range(span // tq):
        acc_w = acc_w + _dot(vwt_ref[sb + k], p_wb[k * tq:(k + 1) * tq, :])

    krow = lax.broadcasted_iota(jnp.int32, (tk, 1), 0)

    def score_tile(kt, masked):
        j0 = pl.multiple_of(kt * tk, tk)
        s = _dot(kp_ref[pl.ds(j0, tk), :], qa_sel)
        if masked:
            s = jnp.where(j0 + krow <= t_row, s, NEG)
        s_ref[...] = s
        return jnp.max(s, axis=0, keepdims=True)

    def finish_tile(kt, m_p, l_p, acc, m_t):
        m_n = jnp.maximum(m_p, m_t)
        p = jnp.exp2(s_ref[...] - m_n)
        alpha = jnp.exp2(m_p - m_n)
        l_n = alpha * l_p + jnp.sum(p, axis=0, keepdims=True)
        acc = alpha * acc + _dot(vst_ref[kt], p.astype(BF16))
        return m_n, l_n, acc

    n_kt = (t0 + tq - 1) // tk + 1

    def pipe_step(kt, carry):
        m_p, l_p, acc, m_t, prev = carry
        m_n, l_n, acc = finish_tile(prev, m_p, l_p, acc, m_t)
        return m_n, l_n, acc, score_tile(kt, False), kt

    carry = (
        jnp.full((1, nl), NEG, F32), jnp.zeros((1, nl), F32), jnp.zeros((HEAD_DIM, nl), F32),
        score_tile(n_kt - 1, True), n_kt - 1,
    )
    m_p, l_p, acc, m_t, prev = lax.fori_loop(0, n_kt - 1, pipe_step, carry)
    _, l_s, acc_s = finish_tile(prev, m_p, l_p, acc, m_t)

    gsc_ref[...] = _t32(_sigmoid(gt_ref[...]))

    def gate_row(br):
        first = br * N_HEADS + g * HPG
        return jnp.concatenate([gsc_ref[pl.ds(first + h, 1), :] for h in range(HPG)], axis=1)

    o_t = (
        gate_row(0) * o_c
        + (gate_row(1) * (1.0 / l_s)) * acc_s
        + (gate_row(2) * (1.0 / l_w)) * acc_w
    )
    for h in range(HPG):
        o_ref[:, h * HEAD_DIM:(h + 1) * HEAD_DIM] = _t32(o_t[:, h * tq:(h + 1) * tq]).astype(o_ref.dtype)


def _nsa_attention(z, kv_cmp, gates):
    bsz, _, seq, _ = z.shape
    tq = min(ATT_Q_TILE, seq)
    tk = min(ATT_K_TILE, seq)
    n_sel = seq // SEL_LEN
    ncp = kv_cmp.shape[3]
    qw = HPG * HEAD_DIM
    kv0 = N_HEADS
    assert n_sel <= AUG_MASK_ROWS and tq & (tq - 1) == 0 and seq % tk == 0 and tk % tq == 0

    cs = np.arange(ncp)[None, :] * CMP_STRIDE
    ss = np.arange(n_sel)[:, None] * SEL_LEN
    ovt = ((cs < ss + SEL_LEN) & (cs + CMP_LEN > ss) & (np.arange(ncp)[None, :] < ncp - 1)).astype(np.float32)

    def aug_cols(pos):
        a = np.zeros((pos.shape[0], LANES), np.float32)
        a[np.arange(pos.shape[0]), np.minimum(pos // SEL_LEN, AUG_MASK_ROWS - 1)] = 1.0
        for k in range(ALIBI_PARTS):
            a[:, AUG_MASK_ROWS + 2 * k] = pos // SEL_LEN
            a[:, AUG_MASK_ROWS + 2 * k + 1] = pos % SEL_LEN
        return jnp.asarray(a, BF16)

    aug = aug_cols(np.arange(seq))
    augc = aug_cols(np.arange(ncp) * CMP_STRIDE + CMP_LEN - 1)
    slopes = jnp.exp2(-8.0 * jnp.arange(1, N_HEADS + 1, dtype=F32) / N_HEADS)

    def kv_spec(col):
        return pl.BlockSpec((None, None, seq, HEAD_DIM), lambda b, g, i, col=col: (b, kv0 + col * N_KV + g, 0, 0))

    const = lambda shape: pl.BlockSpec(shape, lambda b, g, i: (0,) * len(shape))
    kern = functools.partial(_nsa_kernel, tq=tq, tk=tk, seq=seq)
    return pl.pallas_call(
        kern,
        grid=(bsz, N_KV, seq // tq),
        in_specs=[
            pl.BlockSpec(memory_space=pltpu.SMEM),
            pl.BlockSpec((None, HPG, tq, HEAD_DIM), lambda b, g, i: (b, g, i, 0)),
            pl.BlockSpec((None, None, None, ncp, HEAD_DIM), lambda b, g, i: (0, b, g, 0, 0)),
            pl.BlockSpec((None, None, None, ncp, HEAD_DIM), lambda b, g, i: (1, b, g, 0, 0)),
            kv_spec(2), kv_spec(3), kv_spec(4), kv_spec(5),
            pl.BlockSpec((None, tq, LANES), lambda b, g, i: (b, i, 0)),
            const((n_sel, ncp)),
            const((seq, LANES)),
            const((ncp, LANES)),
        ],
        out_specs=pl.BlockSpec((None, tq, qw), lambda b, g, i: (b, i, g)),
        out_shape=jax.ShapeDtypeStruct((bsz, seq, N_HEADS * HEAD_DIM), BF16),
        scratch_shapes=[
            pltpu.VMEM((seq, 2 * HEAD_DIM), BF16),
            pltpu.VMEM((seq, 2 * HEAD_DIM), BF16),
            pltpu.VMEM((ncp, 2 * HEAD_DIM), BF16),
            pltpu.VMEM((seq // tk, HEAD_DIM, tk), BF16),
            pltpu.VMEM((seq // tq, HEAD_DIM, tq), BF16),
            pltpu.VMEM((HEAD_DIM, ncp), BF16),
            pltpu.VMEM((tk, HPG * tq), F32),
            pltpu.VMEM((LANES, tq), F32),
        ],
        compiler_params=_params(3, 5 * seq * HEAD_DIM * 2, 3 * seq * HEAD_DIM * 2, HPG * tq * tk * 12),
        name="nsa_attention",
    )(slopes, z, kv_cmp, kv_cmp, z, z, z, z, gates, jnp.asarray(ovt, BF16), aug, augc)


def _ab_layer(h, bsz, seq, j, ln_g, w_in, conv_w, conv_b, w_r, b_r, w_i, b_i, lam, pool_w, pool_scale, w_out, g_next):
    z = _norm_matmul(h, ln_g, w_in, j, F32)
    y = _mixer(z.reshape(bsz, seq, -1), conv_w, conv_b, w_r, b_r, w_i, b_i, lam, pool_w, pool_scale)
    return _matmul_residual(y.reshape(bsz * seq, -1), w_out, j, h, g_next)


def _nsa_layer(h, hn, bsz, seq, j, w_in, w_gate, pos_k, w1_k, w2_k, pos_v, w1_v, w2_v, w_out, g_next):
    z, gl, zc = _nsa_proj(hn, w_in, j, w_gate, bsz, seq)
    half = CMP_LEN // CMP_STRIDE
    pos = jnp.stack([pos_k, pos_v]).reshape(2, half, 1, CMP_STRIDE * HEAD_DIM)
    w1 = jnp.stack([w1_k, w1_v]).reshape(2, half, CMP_STRIDE * HEAD_DIM, -1).astype(BF16)
    w2 = jnp.stack([w2_k, w2_v]).astype(BF16)
    kv_cmp = _compress(zc, pos, w1, w2)

    o = _nsa_attention(z, kv_cmp, gl.reshape(bsz, seq, LANES))
    return _matmul_residual(o.reshape(bsz * seq, -1), w_out, j, h, g_next)


def kernel(x, p, ln_mix_g, ab_w_in, ab_conv_w, ab_conv_b, ab_w_rgate, ab_b_rgate, ab_w_igate, ab_b_igate, ab_lambda, ab_pool_w, ab_pool_scale, ab_w_out, c_w_in, c_cmp_pos_k, c_cmp_w1_k, c_cmp_w2_k, c_cmp_pos_v, c_cmp_w1_v, c_cmp_w2_v, c_w_out, ln_mlp_g, mlp_w_up, mlp_w_down, ln_ple_g, ple_w_gate, ple_w_proj, ln_final_g):
    bsz, seq, d = x.shape
    depth = p.shape[0]
    h = x.reshape(bsz * seq, d)
    p_all = p.reshape(depth, bsz * seq, -1)
    n_main = N_HEADS * HEAD_DIM + 6 * N_KV * HEAD_DIM
    n_gate = N_BRANCH * N_HEADS
    ab_in, ab_out = ab_w_in.astype(BF16), ab_w_out.astype(BF16)
    c_in, c_out = c_w_in[:, :, :n_main].astype(BF16), c_w_out.astype(BF16)
    w_up, w_down = mlp_w_up.astype(BF16), mlp_w_down.astype(BF16)
    w_pg, w_pp = ple_w_gate.astype(BF16), ple_w_proj.astype(BF16)
    hn = None
    for i in range(depth):
        j = i // 2
        if i % 2 == 0:
            h, hn = _ab_layer(
                h, bsz, seq, j, ln_mix_g[i], ab_in, ab_conv_w[j], ab_conv_b[j], ab_w_rgate[j], ab_b_rgate[j],
                ab_w_igate[j], ab_b_igate[j], ab_lambda[j], ab_pool_w[j], ab_pool_scale[j], ab_out, ln_mlp_g[i],
            )
        else:
            w_gate = jnp.pad(c_w_in[j, :, n_main:], ((0, 0), (0, LANES - n_gate))).astype(BF16)
            h, hn = _nsa_layer(
                h, hn, bsz, seq, j, c_in, w_gate, c_cmp_pos_k[j], c_cmp_w1_k[j], c_cmp_w2_k[j],
                c_cmp_pos_v[j], c_cmp_w1_v[j], c_cmp_w2_v[j], c_out, ln_mlp_g[i],
            )
        h = _mlp(h, hn, w_up, w_down, i)
        if i == depth - 1:
            (h,) = _ple(h, ln_ple_g[i], w_pg, p_all, i, w_pp, ln_final_g, "final")
        elif (i + 1) % 2 == 1:
            h, hn = _ple(h, ln_ple_g[i], w_pg, p_all, i, w_pp, ln_mix_g[i + 1], "next")
        else:
            (h,) = _ple(h, ln_ple_g[i], w_pg, p_all, i, w_pp, ln_final_g, "plain")
    return h.reshape(bsz, seq, d)
```

```python
import functools

import jax
import jax.numpy as jnp
import numpy as np
from jax import lax
from jax.experimental import pallas as pl
from jax.experimental.pallas import tpu as pltpu

F32 = jnp.float32
BF16 = jnp.bfloat16

EPS = 1e-6
NEG = -1e30
FORCE_BONUS = 1e6
LRU_C = 8.0

LANES = 128
V7X_SCOPED_VMEM_CAP = 60000 * 1024
COMPILER_TEMP_BYTES = 12 * 1024 * 1024

LRU_HEADS = 16
POOL_WINDOWS = (2, 4, 8, 16)
CONV_WIDTH = 4
N_HEADS = 16
N_KV = 4
HPG = N_HEADS // N_KV
HEAD_DIM = 128
CMP_LEN = 32
CMP_STRIDE = 16
SEL_LEN = 64
SEL_TOPN = 16
WINDOW = 512
N_BRANCH = 3

ROW_TILE = 1024
COL_TILE = 1024
MLP_ROW_TILE = 512
MLP_FF_TILE = 2048
MLP_FF_CHUNK = 1024
PLE_ROW_TILE = 512
WIDE_ROW_TILE = 512
MIX_TIME_TILE = 256
ATT_Q_TILE = 256
ATT_K_TILE = 512


def _params(n_axes, *block_bytes):
    need = 2 * sum(block_bytes) + COMPILER_TEMP_BYTES
    return pltpu.CompilerParams(
        dimension_semantics=("arbitrary",) * n_axes,
        vmem_limit_bytes=int(min(need, V7X_SCOPED_VMEM_CAP)),
    )


def _rms_bf16(x, g):
    ms = jnp.mean(x * x, axis=-1, keepdims=True)
    return (x * lax.rsqrt(ms + EPS) * g).astype(BF16)


def _sigmoid(x):
    return 1.0 / (1.0 + jnp.exp(-x))


def _gelu_tanh(x):
    c = np.float32(np.sqrt(2.0 / np.pi))
    return x * (0.5 * (1.0 + jnp.tanh(c * (x + 0.044715 * (x * x * x)))))


def _dot(a, b):
    return jnp.dot(a, b, preferred_element_type=F32)


def _dot_nt(a, b):
    return lax.dot_general(a, b, (((1,), (1,)), ((), ())), preferred_element_type=F32)


def _norm_matmul_kernel(x_ref, g_ref, w_ref, o_ref):
    o_ref[...] = _dot(_rms_bf16(x_ref[...], g_ref[...]), w_ref[...]).astype(o_ref.dtype)


def _norm_matmul(x, g, w_all, layer, out_dtype):
    m, k = x.shape
    n = w_all.shape[2]
    tm = min(WIDE_ROW_TILE, m)
    osz = jnp.dtype(out_dtype).itemsize
    return pl.pallas_call(
        _norm_matmul_kernel,
        grid=(m // tm,),
        in_specs=[
            pl.BlockSpec((tm, k), lambda i: (i, 0)),
            pl.BlockSpec((1, k), lambda i: (0, 0)),
            pl.BlockSpec((None, k, n), lambda i: (layer, 0, 0)),
        ],
        out_specs=pl.BlockSpec((tm, n), lambda i: (i, 0)),
        out_shape=jax.ShapeDtypeStruct((m, n), out_dtype),
        compiler_params=_params(1, tm * k * 4, k * n * 2, tm * n * osz),
        name="norm_matmul",
    )(x, g.reshape(1, k), w_all)


def _nsa_proj_kernel(xn_ref, w_ref, wg_ref, z_ref, gl_ref, zc_ref, col_ref, *, cmp_step):
    j = pl.program_id(1)

    @pl.when(j == 0)
    def _():
        gl_ref[...] = _dot(xn_ref[...], wg_ref[...])

    r = _dot(xn_ref[...], w_ref[...])
    rb = r.astype(z_ref.dtype)
    for hb in range(z_ref.shape[0]):
        z_ref[hb] = rb[:, hb * HEAD_DIM:(hb + 1) * HEAD_DIM]

    @pl.when(j == cmp_step)
    def _():
        chunks = zc_ref.shape[1]
        for hb in range(zc_ref.shape[0]):
            col_ref[...] = r[:, hb * HEAD_DIM:(hb + 1) * HEAD_DIM]
            for l in range(CMP_STRIDE):
                rows = col_ref[pl.ds(l, chunks, stride=CMP_STRIDE), :]
                zc_ref[hb, :, l * HEAD_DIM:(l + 1) * HEAD_DIM] = rows.astype(zc_ref.dtype)


def _nsa_proj(xn, w_all, layer, w_gate, bsz, seq):
    m, k = xn.shape
    n_main = w_all.shape[2]
    tm = min(ROW_TILE, seq)
    tn = COL_TILE
    hpb = tn // HEAD_DIM
    spb = seq // tm
    n_cmp = 2 * N_KV
    assert hpb == n_cmp and (N_HEADS * HEAD_DIM) % tn == 0
    cmp_step = N_HEADS * HEAD_DIM // tn
    return pl.pallas_call(
        functools.partial(_nsa_proj_kernel, cmp_step=cmp_step),
        grid=(m // tm, n_main // tn),
        in_specs=[
            pl.BlockSpec((tm, k), lambda i, j: (i, 0)),
            pl.BlockSpec((None, k, tn), lambda i, j: (layer, 0, j)),
            pl.BlockSpec((k, LANES), lambda i, j: (0, 0)),
        ],
        out_specs=[
            pl.BlockSpec((None, hpb, tm, HEAD_DIM), lambda i, j: (i // spb, j, i % spb, 0)),
            pl.BlockSpec((tm, LANES), lambda i, j: (i, 0)),
            pl.BlockSpec(
                (None, n_cmp, tm // CMP_STRIDE, CMP_STRIDE * HEAD_DIM), lambda i, j: (i // spb, 0, i % spb, 0)
            ),
        ],
        out_shape=[
            jax.ShapeDtypeStruct((bsz, n_main // HEAD_DIM, seq, HEAD_DIM), BF16),
            jax.ShapeDtypeStruct((m, LANES), F32),
            jax.ShapeDtypeStruct((bsz, n_cmp, seq // CMP_STRIDE, CMP_STRIDE * HEAD_DIM), BF16),
        ],
        scratch_shapes=[pltpu.VMEM((tm, HEAD_DIM), F32)],
        compiler_params=_params(
            2, tm * k * 2, k * tn * 2, k * LANES * 2, tm * tn * 2, tm * LANES * 4, tm * tn * 2, tm * tn * 4
        ),
        name="nsa_proj",
    )(xn, w_all, w_gate)


def _matmul_residual_kernel(y_ref, w_ref, h_ref, gn_ref, o_ref, on_ref):
    h = h_ref[...] + _dot(y_ref[...], w_ref[...])
    o_ref[...] = h
    on_ref[...] = _rms_bf16(h, gn_ref[...])


def _matmul_residual(y, w_all, layer, h, g_next):
    m, k = y.shape
    n = w_all.shape[2]
    tm = min(WIDE_ROW_TILE, m)
    return pl.pallas_call(
        _matmul_residual_kernel,
        grid=(m // tm,),
        in_specs=[
            pl.BlockSpec((tm, k), lambda i: (i, 0)),
            pl.BlockSpec((None, k, n), lambda i: (layer, 0, 0)),
            pl.BlockSpec((tm, n), lambda i: (i, 0)),
            pl.BlockSpec((1, n), lambda i: (0, 0)),
        ],
        out_specs=[pl.BlockSpec((tm, n), lambda i: (i, 0)), pl.BlockSpec((tm, n), lambda i: (i, 0))],
        out_shape=[jax.ShapeDtypeStruct((m, n), F32), jax.ShapeDtypeStruct((m, n), BF16)],
        compiler_params=_params(1, tm * k * 2, k * n * 2, tm * n * 4, tm * n * 4, tm * n * 2),
        name="matmul_residual",
    )(y, w_all, h, g_next.reshape(1, n))


def _mlp_kernel(x_ref, xn_ref, wu_ref, wd_ref, o_ref, *, chunk):
    @pl.when(pl.program_id(1) == 0)
    def _():
        o_ref[...] = x_ref[...]

    xn = xn_ref[...]
    for c0 in range(0, wu_ref.shape[1], chunk):
        u = _dot(xn, wu_ref[:, c0:c0 + chunk])
        u = jnp.square(jnp.maximum(u, 0.0)).astype(BF16)
        o_ref[...] += _dot(u, wd_ref[c0:c0 + chunk, :])


def _mlp(x, xn, w_up, w_down, layer):
    m, d = x.shape
    f = w_up.shape[2]
    tm = min(MLP_ROW_TILE, m)
    tf = min(MLP_FF_TILE, f)
    chunk = min(MLP_FF_CHUNK, tf)
    return pl.pallas_call(
        functools.partial(_mlp_kernel, chunk=chunk),
        grid=(m // tm, f // tf),
        in_specs=[
            pl.BlockSpec((tm, d), lambda i, j: (i, 0)),
            pl.BlockSpec((tm, d), lambda i, j: (i, 0)),
            pl.BlockSpec((None, d, tf), lambda i, j: (layer, 0, j)),
            pl.BlockSpec((None, tf, d), lambda i, j: (layer, j, 0)),
        ],
        out_specs=pl.BlockSpec((tm, d), lambda i, j: (i, 0)),
        out_shape=jax.ShapeDtypeStruct((m, d), F32),
        compiler_params=_params(2, tm * d * 4, tm * d * 2, d * tf * 2, tf * d * 2, tm * d * 4),
        name="mlp",
    )(x, xn, w_up, w_down)


def _ple_kernel(x_ref, g_ref, wg_ref, p_ref, wp_ref, gn_ref, o_ref, *, mode):
    x = x_ref[...]
    gate = _sigmoid(_dot(_rms_bf16(x, g_ref[...]), wg_ref[...]))
    proj = _dot(p_ref[...].astype(BF16), wp_ref[...])
    h = x + gate * proj
    if mode == "final":
        ms = jnp.mean(h * h, axis=-1, keepdims=True)
        o_ref[0][...] = h * lax.rsqrt(ms + EPS) * gn_ref[...]
    else:
        o_ref[0][...] = h
        if mode == "next":
            o_ref[1][...] = _rms_bf16(h, gn_ref[...])


def _ple(x, g, w_gate, p_all, layer, w_proj, g_next, mode):
    m, d = x.shape
    e = p_all.shape[-1]
    tm = min(PLE_ROW_TILE, m)
    const = lambda shape: pl.BlockSpec(shape, lambda i: (0,) * len(shape))
    row_spec = pl.BlockSpec((tm, d), lambda i: (i, 0))
    out_specs, out_shape = [row_spec], [jax.ShapeDtypeStruct((m, d), F32)]
    if mode == "next":
        out_specs.append(row_spec)
        out_shape.append(jax.ShapeDtypeStruct((m, d), BF16))

    def body(*refs):
        _ple_kernel(*refs[:6], refs[6:], mode=mode)

    return pl.pallas_call(
        body,
        grid=(m // tm,),
        in_specs=[
            row_spec,
            const((1, d)),
            pl.BlockSpec((None, d, d), lambda i: (layer, 0, 0)),
            pl.BlockSpec((None, tm, e), lambda i: (layer, i, 0)),
            pl.BlockSpec((None, e, d), lambda i: (layer, 0, 0)),
            const((1, d)),
        ],
        out_specs=out_specs,
        out_shape=out_shape,
        compiler_params=_params(1, tm * d * 4, d * d * 2, tm * e * 4, e * d * 2, tm * d * 4, tm * d * 8),
        name="ple",
    )(x, g.reshape(1, d), w_gate, p_all, w_proj, g_next.reshape(1, d))


def _mixer_kernel(
    z_ref, cw_ref, cb_ref, wr_ref, br_ref, wi_ref, bi_ref, lam_ref, pw_ref, ps_ref,
    y_ref, xext_ref, uext_ref, hlast_ref, *, ts, width,
):
    s = pl.program_id(1)
    xh = 8
    uh = 16

    @pl.when(s == 0)
    def _():
        xext_ref[0:xh, :] = jnp.zeros((xh, width), F32)
        uext_ref[0:uh, :] = jnp.zeros((uh, width), F32)
        hlast_ref[...] = jnp.zeros_like(hlast_ref)

    xr = z_ref[:, 0:width]
    gate = z_ref[:, width:2 * width]
    u = z_ref[:, 2 * width:3 * width]

    xext_ref[xh:xh + ts, :] = xr
    conv = cb_ref[...] + cw_ref[CONV_WIDTH - 1:CONV_WIDTH, :] * xr
    for k in range(CONV_WIDTH - 1):
        shift = CONV_WIDTH - 1 - k
        conv = conv + cw_ref[k:k + 1, :] * xext_ref[pl.ds(xh - shift, ts), :]
    xext_ref[0:xh, :] = xr[ts - xh:ts, :]

    gw = wr_ref.shape[1]
    conv_b = conv.astype(BF16)
    r_parts, i_parts = [], []
    for c in range(width // gw):
        xc = conv_b[:, c * gw:(c + 1) * gw]
        r_parts.append(_dot(xc, wr_ref[c]))
        i_parts.append(_dot(xc, wi_ref[c]))
    r = _sigmoid(jnp.concatenate(r_parts, axis=1) + br_ref[...])
    ig = _sigmoid(jnp.concatenate(i_parts, axis=1) + bi_ref[...])

    nl = -lam_ref[...]
    softplus = jnp.maximum(nl, 0.0) + jnp.log(1.0 + jnp.exp(-jnp.abs(nl)))
    log_a = -LRU_C * r * softplus
    a = jnp.exp(log_a)
    b = jnp.sqrt(1.0 - a * a) * (ig * conv)

    sub = 8
    srow = lax.broadcasted_iota(jnp.int32, (sub, 1), 0)
    carry = hlast_ref[...]
    h_groups = []
    for r0 in range(0, ts, sub):
        ag = a[r0:r0 + sub, :]
        bg = b[r0:r0 + sub, :]
        d = 1
        while d < sub:
            keep = srow >= d
            a_sh = jnp.where(keep, pltpu.roll(ag, d, axis=0), 1.0)
            b_sh = jnp.where(keep, pltpu.roll(bg, d, axis=0), 0.0)
            bg = ag * b_sh + bg
            ag = ag * a_sh
            d *= 2
        hg = ag * carry + bg
        carry = hg[sub - 1:sub, :]
        h_groups.append(hg)
    h = jnp.concatenate(h_groups, axis=0)
    hlast_ref[...] = carry
    y_ref[:, 0:width] = (h * _gelu_tanh(gate)).astype(BF16)

    uext_ref[uh:uh + ts, :] = u
    t = s * ts + lax.broadcasted_iota(jnp.int32, (ts, 1), 0)
    gd = width // len(POOL_WINDOWS)
    for gi, win in enumerate(POOL_WINDOWS):
        cs = slice(gi * gd, (gi + 1) * gd)
        wsum = u[:, cs]
        for k in range(1, win):
            wsum = wsum + uext_ref[pl.ds(uh - k, ts), cs]
        cnt = jnp.minimum(t + 1, win).astype(F32)
        dd = wsum / cnt - u[:, cs]
        yp = _dot(dd.astype(BF16), pw_ref[gi]) * ps_ref[:, cs]
        y_ref[:, width + gi * gd:width + (gi + 1) * gd] = yp.astype(BF16)
    uext_ref[0:uh, :] = u[ts - uh:ts, :]


def _mixer(z, conv_w, conv_b, w_r, b_r, w_i, b_i, lam, pool_w, pool_scale):
    bsz, seq, w3 = z.shape
    width = w3 // 3
    ts = min(MIX_TIME_TILE, seq)
    hd = width // LRU_HEADS
    per = 2 * LANES // hd
    gw = per * hd

    def block_diag(w):
        w = w.reshape(LRU_HEADS // per, per, hd, hd)
        eye = jnp.eye(per, dtype=w.dtype)
        return jnp.einsum("cpij,pq->cpiqj", w, eye).reshape(LRU_HEADS // per, gw, gw).astype(BF16)

    row = lambda v: v.reshape(1, width)
    const2 = lambda shape: pl.BlockSpec(shape, lambda b, s: (0,) * len(shape))
    kern = functools.partial(_mixer_kernel, ts=ts, width=width)
    gd = width // len(POOL_WINDOWS)
    return pl.pallas_call(
        kern,
        grid=(bsz, seq // ts),
        in_specs=[
            pl.BlockSpec((None, ts, w3), lambda b, s: (b, s, 0)),
            const2((CONV_WIDTH, width)),
            const2((1, width)),
            const2((LRU_HEADS // per, gw, gw)),
            const2((1, width)),
            const2((LRU_HEADS // per, gw, gw)),
            const2((1, width)),
            const2((1, width)),
            const2((len(POOL_WINDOWS), gd, gd)),
            const2((1, width)),
        ],
        out_specs=pl.BlockSpec((None, ts, 2 * width), lambda b, s: (b, s, 0)),
        out_shape=jax.ShapeDtypeStruct((bsz, seq, 2 * width), BF16),
        scratch_shapes=[
            pltpu.VMEM((ts + 8, width), F32),
            pltpu.VMEM((ts + 16, width), F32),
            pltpu.VMEM((1, width), F32),
        ],
        compiler_params=_params(2, ts * w3 * 4, ts * width * 4, ts * width * 16),
        name="lru_pool_mixer",
    )(
        z, conv_w, row(conv_b), block_diag(w_r), row(b_r), block_diag(w_i), row(b_i), row(lam),
        pool_w.astype(BF16), row(pool_scale),
    )


def _compress_kernel(x_ref, pos_ref, w1_ref, w2_ref, o_ref):
    x = x_ref[...].astype(F32)
    xa = (x + pos_ref[0]).astype(BF16)
    xb = (x + pos_ref[1]).astype(BF16)
    first = _dot(xa, w1_ref[0])
    second = _dot(xb, w1_ref[1])
    n = x.shape[0]
    hid = first + pltpu.roll(second, n - 1, axis=0)
    o_ref[...] = _dot(_gelu_tanh(hid).astype(BF16), w2_ref[...]).astype(o_ref.dtype)


def _compress(zc, pos, w1, w2):
    bsz, _, nc, kk = zc.shape
    two, ng = 2, N_KV
    hid = w1.shape[-1]
    dh = w2.shape[-1]
    return pl.pallas_call(
        _compress_kernel,
        grid=(two, bsz, ng),
        in_specs=[
            pl.BlockSpec((None, None, nc, kk), lambda t, b, g: (b, t * N_KV + g, 0, 0)),
            pl.BlockSpec((None, 2, 1, kk), lambda t, b, g: (t, 0, 0, 0)),
            pl.BlockSpec((None, 2, kk, hid), lambda t, b, g: (t, 0, 0, 0)),
            pl.BlockSpec((None, hid, dh), lambda t, b, g: (t, 0, 0)),
        ],
        out_specs=pl.BlockSpec((None, None, None, nc, dh), lambda t, b, g: (t, b, g, 0, 0)),
        out_shape=jax.ShapeDtypeStruct((two, bsz, ng, nc, dh), BF16),
        compiler_params=_params(3, nc * kk * 2, 2 * kk * hid * 2, hid * dh * 2, nc * dh * 2, nc * kk * 8),
        name="nsa_compress",
    )(zc, pos, w1, w2)


LOG2E = np.float32(1.4426950408889634)
AUG_MASK_ROWS = 64
ALIBI_PARTS = 3


def _t32(x):
    r, c = x.shape
    rows = []
    for cb in range(0, c, LANES):
        rows.append(jnp.concatenate([x[rb:rb + LANES, cb:cb + LANES].T for rb in range(0, r, LANES)], axis=1))
    return jnp.concatenate(rows, axis=0)


def _t_bf16(x):
    return _t32(x.astype(F32)).astype(BF16)


def _nsa_kernel(
    slope_ref, q_ref, kc_ref, vc_ref, ks_ref, vs_ref, kw_ref, vw_ref, gt_ref, ovt_ref, aug_ref, augc_ref,
    o_ref, kp_ref, kwp_ref, kcp_ref, vst_ref, vwt_ref, vct_ref, s_ref, gsc_ref, *, tq, tk, seq,
):
    g = pl.program_id(1)
    i = pl.program_id(2)
    t0 = i * tq
    nl = HPG * tq
    n_sel = seq // SEL_LEN
    ncp = kc_ref.shape[0]

    @pl.when(i == 0)
    def _():
        kp_ref[:, 0:HEAD_DIM] = ks_ref[...]
        kp_ref[:, HEAD_DIM:2 * HEAD_DIM] = aug_ref[...]
        kwp_ref[:, 0:HEAD_DIM] = kw_ref[...]
        kwp_ref[:, HEAD_DIM:2 * HEAD_DIM] = aug_ref[...]
        kcp_ref[:, 0:HEAD_DIM] = kc_ref[...]
        kcp_ref[:, HEAD_DIM:2 * HEAD_DIM] = augc_ref[...]
        for kt in range(seq // tk):
            vst_ref[kt] = _t_bf16(vs_ref[kt * tk:(kt + 1) * tk, :])
        for kb in range(seq // tq):
            vwt_ref[kb] = _t_bf16(vw_ref[kb * tq:(kb + 1) * tq, :])
        vct_ref[...] = _t_bf16(vc_ref[...])

    qscale = np.float32(HEAD_DIM ** -0.5) * LOG2E
    q_t = jnp.concatenate(
        [_t32(q_ref[h].astype(F32) * qscale) for h in range(HPG)], axis=1
    ).astype(BF16)

    lane = lax.broadcasted_iota(jnp.int32, (1, nl), 1)
    slope = jnp.zeros((1, nl), F32)
    for h in range(HPG):
        in_h = (lane >= h * tq) & (lane < (h + 1) * tq)
        slope = jnp.where(in_h, slope_ref[g * HPG + h] * LOG2E, slope)
    t_row = t0 + (lane & (tq - 1))

    arow = lax.broadcasted_iota(jnp.int32, (LANES - AUG_MASK_ROWS, nl), 0)
    alibi_rows = jnp.zeros((LANES - AUG_MASK_ROWS, nl), F32)
    rem = slope
    for k in range(ALIBI_PARTS):
        piece = rem.astype(BF16).astype(F32)
        rem = rem - piece
        alibi_rows = jnp.where(arow == 2 * k, piece * float(SEL_LEN), alibi_rows)
        alibi_rows = jnp.where(arow == 2 * k + 1, piece, alibi_rows)
    qa_plain = jnp.concatenate(
        [q_t, jnp.zeros((AUG_MASK_ROWS, nl), BF16), alibi_rows.astype(BF16)], axis=0
    )

    cend = lax.broadcasted_iota(jnp.int32, (ncp, 1), 0) * CMP_STRIDE + (CMP_LEN - 1)
    valid_c = cend <= t_row
    s_c = jnp.where(valid_c, _dot(kcp_ref[...], qa_plain), NEG)
    m_c = jnp.max(s_c, axis=0, keepdims=True)
    p_c = jnp.where(valid_c, jnp.exp2(s_c - m_c), 0.0)
    l_c = jnp.sum(p_c, axis=0, keepdims=True)
    p_c = p_c * (1.0 / jnp.where(l_c > 0.0, l_c, 1.0))
    o_c = _dot(vct_ref[...], p_c.astype(BF16))

    p_sum = p_c[:, 0:tq]
    for h in range(1, HPG):
        p_sum = p_sum + p_c[:, h * tq:(h + 1) * tq]
    ovt = ovt_ref[...]
    top_n = min(SEL_TOPN, n_sel)
    sub = 8
    mask_cols = []
    for c0 in range(0, tq, LANES):
        imp = jnp.zeros((n_sel, LANES), F32)
        rem = p_sum[:, c0:c0 + LANES]
        for _ in range(3):
            piece = rem.astype(BF16)
            rem = rem - piece.astype(F32)
            imp = imp + _dot(ovt, piece)
        blk = lax.broadcasted_iota(jnp.int32, (n_sel, LANES), 0)
        tl = t0 + c0 + lax.broadcasted_iota(jnp.int32, (n_sel, LANES), 1)
        cur = lax.shift_right_arithmetic(tl, int(np.log2(SEL_LEN)))
        forced = (blk == 0) | (blk == cur) | (blk == cur - 1)
        causal_s = blk * SEL_LEN <= tl
        v = jnp.where(forced, imp + FORCE_BONUS, imp)
        v = jnp.where(causal_s, v, NEG)
        rank_slabs = []
        for r0 in range(0, n_sel, sub):
            vs_ = v[r0:r0 + sub, :]
            lower = blk[r0:r0 + sub, :]
            rk = jnp.zeros((sub, LANES), F32)
            for j in range(n_sel):
                vj = v[j:j + 1, :]
                if j < r0:
                    beats = vj >= vs_
                elif j >= r0 + sub - 1:
                    beats = vj > vs_
                else:
                    beats = (vj > vs_) | ((vj == vs_) & (lower > j))
                rk = rk + jnp.where(beats, 1.0, 0.0)
            rank_slabs.append(rk)
        rank = jnp.concatenate(rank_slabs, axis=0)
        mask_cols.append(jnp.where((rank < top_n) & causal_s, 0.0, NEG))
    mask_rows = mask_cols[0] if len(mask_cols) == 1 else jnp.concatenate(mask_cols, axis=1)
    if n_sel < AUG_MASK_ROWS:
        mask_rows = jnp.concatenate([mask_rows, jnp.zeros((AUG_MASK_ROWS - n_sel, tq), F32)], axis=0)
    mask_rows = jnp.concatenate([mask_rows] * HPG, axis=1).astype(BF16)
    qa_sel = jnp.concatenate([q_t, mask_rows, alibi_rows.astype(BF16)], axis=0)

    span = WINDOW + tq
    start = pl.multiple_of(jnp.maximum(t0 - WINDOW, 0), tq)
    dist = t_row - (start + lax.broadcasted_iota(jnp.int32, (span, 1), 0))
    valid_w = lax.bitcast_convert_type(dist, jnp.uint32) < jnp.uint32(WINDOW)
    s_w = jnp.where(valid_w, _dot(kwp_ref[pl.ds(start, span), :], qa_plain), NEG)
    m_w = jnp.max(s_w, axis=0, keepdims=True)
    p_w = jnp.exp2(s_w - m_w)
    l_w = jnp.sum(p_w, axis=0, keepdims=True)
    p_wb = p_w.astype(BF16)
    sb = start // tq
    acc_w = jnp.zeros((HEAD_DIM, nl), F32)
    for k in range(span // tq):
        acc_w = acc_w + _dot(vwt_ref[sb + k], p_wb[k * tq:(k + 1) * tq, :])

    krow = lax.broadcasted_iota(jnp.int32, (tk, 1), 0)

    def score_tile(kt, masked):
        j0 = pl.multiple_of(kt * tk, tk)
        s = _dot(kp_ref[pl.ds(j0, tk), :], qa_sel)
        if masked:
            s = jnp.where(j0 + krow <= t_row, s, NEG)
        s_ref[...] = s
        return jnp.max(s, axis=0, keepdims=True)

    def finish_tile(kt, m_p, l_p, acc, m_t):
        m_n = jnp.maximum(m_p, m_t)
        p = jnp.exp2(s_ref[...] - m_n)
        alpha = jnp.exp2(m_p - m_n)
        l_n = alpha * l_p + jnp.sum(p, axis=0, keepdims=True)
        acc = alpha * acc + _dot(vst_ref[kt], p.astype(BF16))
        return m_n, l_n, acc

    n_kt = (t0 + tq - 1) // tk + 1

    def pipe_step(kt, carry):
        m_p, l_p, acc, m_t, prev = carry
        m_n, l_n, acc = finish_tile(prev, m_p, l_p, acc, m_t)
        return m_n, l_n, acc, score_tile(kt, False), kt

    carry = (
        jnp.full((1, nl), NEG, F32), jnp.zeros((1, nl), F32), jnp.zeros((HEAD_DIM, nl), F32),
        score_tile(n_kt - 1, True), n_kt - 1,
    )
    m_p, l_p, acc, m_t, prev = lax.fori_loop(0, n_kt - 1, pipe_step, carry)
    _, l_s, acc_s = finish_tile(prev, m_p, l_p, acc, m_t)

    gsc_ref[...] = _t32(_sigmoid(gt_ref[...]))

    def gate_row(br):
        first = br * N_HEADS + g * HPG
        return jnp.concatenate([gsc_ref[pl.ds(first + h, 1), :] for h in range(HPG)], axis=1)

    o_t = (
        gate_row(0) * o_c
        + (gate_row(1) * (1.0 / l_s)) * acc_s
        + (gate_row(2) * (1.0 / l_w)) * acc_w
    )
    for h in range(HPG):
        o_ref[:, h * HEAD_DIM:(h + 1) * HEAD_DIM] = _t32(o_t[:, h * tq:(h + 1) * tq]).astype(o_ref.dtype)


def _nsa_attention(z, kv_cmp, gates):
    bsz, _, seq, _ = z.shape
    tq = min(ATT_Q_TILE, seq)
    tk = min(ATT_K_TILE, seq)
    n_sel = seq // SEL_LEN
    ncp = kv_cmp.shape[3]
    qw = HPG * HEAD_DIM
    kv0 = N_HEADS
    assert n_sel <= AUG_MASK_ROWS and tq & (tq - 1) == 0 and seq % tk == 0 and tk % tq == 0

    cs = np.arange(ncp)[None, :] * CMP_STRIDE
    ss = np.arange(n_sel)[:, None] * SEL_LEN
    ovt = ((cs < ss + SEL_LEN) & (cs + CMP_LEN > ss) & (np.arange(ncp)[None, :] < ncp - 1)).astype(np.float32)

    def aug_cols(pos):
        a = np.zeros((pos.shape[0], LANES), np.float32)
        a[np.arange(pos.shape[0]), np.minimum(pos // SEL_LEN, AUG_MASK_ROWS - 1)] = 1.0
        for k in range(ALIBI_PARTS):
            a[:, AUG_MASK_ROWS + 2 * k] = pos // SEL_LEN
            a[:, AUG_MASK_ROWS + 2 * k + 1] = pos % SEL_LEN
        return jnp.asarray(a, BF16)

    aug = aug_cols(np.arange(seq))
    augc = aug_cols(np.arange(ncp) * CMP_STRIDE + CMP_LEN - 1)
    slopes = jnp.exp2(-8.0 * jnp.arange(1, N_HEADS + 1, dtype=F32) / N_HEADS)

    def kv_spec(col):
        return pl.BlockSpec((None, None, seq, HEAD_DIM), lambda b, g, i, col=col: (b, kv0 + col * N_KV + g, 0, 0))

    const = lambda shape: pl.BlockSpec(shape, lambda b, g, i: (0,) * len(shape))
    kern = functools.partial(_nsa_kernel, tq=tq, tk=tk, seq=seq)
    return pl.pallas_call(
        kern,
        grid=(bsz, N_KV, seq // tq),
        in_specs=[
            pl.BlockSpec(memory_space=pltpu.SMEM),
            pl.BlockSpec((None, HPG, tq, HEAD_DIM), lambda b, g, i: (b, g, i, 0)),
            pl.BlockSpec((None, None, None, ncp, HEAD_DIM), lambda b, g, i: (0, b, g, 0, 0)),
            pl.BlockSpec((None, None, None, ncp, HEAD_DIM), lambda b, g, i: (1, b, g, 0, 0)),
            kv_spec(2), kv_spec(3), kv_spec(4), kv_spec(5),
            pl.BlockSpec((None, tq, LANES), lambda b, g, i: (b, i, 0)),
            const((n_sel, ncp)),
            const((seq, LANES)),
            const((ncp, LANES)),
        ],
        out_specs=pl.BlockSpec((None, tq, qw), lambda b, g, i: (b, i, g)),
        out_shape=jax.ShapeDtypeStruct((bsz, seq, N_HEADS * HEAD_DIM), BF16),
        scratch_shapes=[
            pltpu.VMEM((seq, 2 * HEAD_DIM), BF16),
            pltpu.VMEM((seq, 2 * HEAD_DIM), BF16),
            pltpu.VMEM((ncp, 2 * HEAD_DIM), BF16),
            pltpu.VMEM((seq // tk, HEAD_DIM, tk), BF16),
            pltpu.VMEM((seq // tq, HEAD_DIM, tq), BF16),
            pltpu.VMEM((HEAD_DIM, ncp), BF16),
            pltpu.VMEM((tk, HPG * tq), F32),
            pltpu.VMEM((LANES, tq), F32),
        ],
        compiler_params=_params(3, 5 * seq * HEAD_DIM * 2, 3 * seq * HEAD_DIM * 2, HPG * tq * tk * 12),
        name="nsa_attention",
    )(slopes, z, kv_cmp, kv_cmp, z, z, z, z, gates, jnp.asarray(ovt, BF16), aug, augc)


def _ab_layer(h, bsz, seq, j, ln_g, w_in, conv_w, conv_b, w_r, b_r, w_i, b_i, lam, pool_w, pool_scale, w_out, g_next):
    z = _norm_matmul(h, ln_g, w_in, j, F32)
    y = _mixer(z.reshape(bsz, seq, -1), conv_w, conv_b, w_r, b_r, w_i, b_i, lam, pool_w, pool_scale)
    return _matmul_residual(y.reshape(bsz * seq, -1), w_out, j, h, g_next)


def _nsa_layer(h, hn, bsz, seq, j, w_in, w_gate, pos_k, w1_k, w2_k, pos_v, w1_v, w2_v, w_out, g_next):
    z, gl, zc = _nsa_proj(hn, w_in, j, w_gate, bsz, seq)
    half = CMP_LEN // CMP_STRIDE
    pos = jnp.stack([pos_k, pos_v]).reshape(2, half, 1, CMP_STRIDE * HEAD_DIM)
    w1 = jnp.stack([w1_k, w1_v]).reshape(2, half, CMP_STRIDE * HEAD_DIM, -1).astype(BF16)
    w2 = jnp.stack([w2_k, w2_v]).astype(BF16)
    kv_cmp = _compress(zc, pos, w1, w2)

    o = _nsa_attention(z, kv_cmp, gl.reshape(bsz, seq, LANES))
    return _matmul_residual(o.reshape(bsz * seq, -1), w_out, j, h, g_next)


def kernel(x, p, ln_mix_g, ab_w_in, ab_conv_w, ab_conv_b, ab_w_rgate, ab_b_rgate, ab_w_igate, ab_b_igate, ab_lambda, ab_pool_w, ab_pool_scale, ab_w_out, c_w_in, c_cmp_pos_k, c_cmp_w1_k, c_cmp_w2_k, c_cmp_pos_v, c_cmp_w1_v, c_cmp_w2_v, c_w_out, ln_mlp_g, mlp_w_up, mlp_w_down, ln_ple_g, ple_w_gate, ple_w_proj, ln_final_g):
    bsz, seq, d = x.shape
    depth = p.shape[0]
    h = x.reshape(bsz * seq, d)
    p_all = p.reshape(depth, bsz * seq, -1)
    n_main = N_HEADS * HEAD_DIM + 6 * N_KV * HEAD_DIM
    n_gate = N_BRANCH * N_HEADS
    ab_in, ab_out = ab_w_in.astype(BF16), ab_w_out.astype(BF16)
    c_in, c_out = c_w_in[:, :, :n_main].astype(BF16), c_w_out.astype(BF16)
    w_up, w_down = mlp_w_up.astype(BF16), mlp_w_down.astype(BF16)
    w_pg, w_pp = ple_w_gate.astype(BF16), ple_w_proj.astype(BF16)
    hn = None
    for i in range(depth):
        j = i // 2
        if i % 2 == 0:
            h, hn = _ab_layer(
                h, bsz, seq, j, ln_mix_g[i], ab_in, ab_conv_w[j], ab_conv_b[j], ab_w_rgate[j], ab_b_rgate[j],
                ab_w_igate[j], ab_b_igate[j], ab_lambda[j], ab_pool_w[j], ab_pool_scale[j], ab_out, ln_mlp_g[i],
            )
        else:
            w_gate = jnp.pad(c_w_in[j, :, n_main:], ((0, 0), (0, LANES - n_gate))).astype(BF16)
            h, hn = _nsa_layer(
                h, hn, bsz, seq, j, c_in, w_gate, c_cmp_pos_k[j], c_cmp_w1_k[j], c_cmp_w2_k[j],
                c_cmp_pos_v[j], c_cmp_w1_v[j], c_cmp_w2_v[j], c_out, ln_mlp_g[i],
            )
        h = _mlp(h, hn, w_up, w_down, i)
        if i == depth - 1:
            (h,) = _ple(h, ln_ple_g[i], w_pg, p_all, i, w_pp, ln_final_g, "final")
        elif (i + 1) % 2 == 1:
            h, hn = _ple(h, ln_ple_g[i], w_pg, p_all, i, w_pp, ln_mix_g[i + 1], "next")
        else:
            (h,) = _ple(h, ln_ple_g[i], w_pg, p_all, i, w_pp, ln_final_g, "plain")
    return h.reshape(bsz, seq, d)
```

```python
import functools

import jax
import jax.numpy as jnp
import numpy as np
from jax import lax
from jax.experimental import pallas as pl
from jax.experimental.pallas import tpu as pltpu

F32 = jnp.float32
BF16 = jnp.bfloat16

EPS = 1e-6
NEG = -1e30
FORCE_BONUS = 1e6
LRU_C = 8.0

LANES = 128
V7X_SCOPED_VMEM_CAP = 60000 * 1024
COMPILER_TEMP_BYTES = 12 * 1024 * 1024

LRU_HEADS = 16
POOL_WINDOWS = (2, 4, 8, 16)
CONV_WIDTH = 4
N_HEADS = 16
N_KV = 4
HPG = N_HEADS // N_KV
HEAD_DIM = 128
CMP_LEN = 32
CMP_STRIDE = 16
SEL_LEN = 64
SEL_TOPN = 16
WINDOW = 512
N_BRANCH = 3

ROW_TILE = 1024
COL_TILE = 1024
MLP_ROW_TILE = 512
MLP_FF_TILE = 2048
MLP_FF_CHUNK = 1024
PLE_ROW_TILE = 512
WIDE_ROW_TILE = 512
MIX_TIME_TILE = 256
ATT_Q_TILE = 256
ATT_K_TILE = 512


def _params(n_axes, *block_bytes):
    need = 2 * sum(block_bytes) + COMPILER_TEMP_BYTES
    return pltpu.CompilerParams(
        dimension_semantics=("arbitrary",) * n_axes,
        vmem_limit_bytes=int(min(need, V7X_SCOPED_VMEM_CAP)),
    )


def _rms_bf16(x, g):
    ms = jnp.mean(x * x, axis=-1, keepdims=True)
    return (x * lax.rsqrt(ms + EPS) * g).astype(BF16)


def _sigmoid(x):
    return 1.0 / (1.0 + jnp.exp(-x))


def _gelu_tanh(x):
    c = np.float32(np.sqrt(2.0 / np.pi))
    return x * (0.5 * (1.0 + jnp.tanh(c * (x + 0.044715 * (x * x * x)))))


def _dot(a, b):
    return jnp.dot(a, b, preferred_element_type=F32)


def _norm_matmul_kernel(x_ref, g_ref, w_ref, o_ref):
    o_ref[...] = _dot(_rms_bf16(x_ref[...], g_ref[...]), w_ref[...]).astype(o_ref.dtype)


def _norm_matmul(x, g, w_all, layer, out_dtype):
    m, k = x.shape
    n = w_all.shape[2]
    tm = min(WIDE_ROW_TILE, m)
    osz = jnp.dtype(out_dtype).itemsize
    return pl.pallas_call(
        _norm_matmul_kernel,
        grid=(m // tm,),
        in_specs=[
            pl.BlockSpec((tm, k), lambda i: (i, 0)),
            pl.BlockSpec((1, k), lambda i: (0, 0)),
            pl.BlockSpec((None, k, n), lambda i: (layer, 0, 0)),
        ],
        out_specs=pl.BlockSpec((tm, n), lambda i: (i, 0)),
        out_shape=jax.ShapeDtypeStruct((m, n), out_dtype),
        compiler_params=_params(1, tm * k * 4, k * n * 2, tm * n * osz),
        name="norm_matmul",
    )(x, g.reshape(1, k), w_all)


def _nsa_proj_kernel(xn_ref, w_ref, wg_ref, z_ref, gl_ref, zc_ref, col_ref, *, cmp_step):
    j = pl.program_id(1)

    @pl.when(j == 0)
    def _():
        gl_ref[...] = _dot(xn_ref[...], wg_ref[...])

    r = _dot(xn_ref[...], w_ref[...])
    rb = r.astype(z_ref.dtype)
    for hb in range(z_ref.shape[0]):
        z_ref[hb] = rb[:, hb * HEAD_DIM:(hb + 1) * HEAD_DIM]

    @pl.when(j == cmp_step)
    def _():
        chunks = zc_ref.shape[1]
        for hb in range(zc_ref.shape[0]):
            col_ref[...] = r[:, hb * HEAD_DIM:(hb + 1) * HEAD_DIM]
            for l in range(CMP_STRIDE):
                rows = col_ref[pl.ds(l, chunks, stride=CMP_STRIDE), :]
                zc_ref[hb, :, l * HEAD_DIM:(l + 1) * HEAD_DIM] = rows.astype(zc_ref.dtype)


def _nsa_proj(xn, w_all, layer, w_gate, bsz, seq):
    m, k = xn.shape
    n_main = w_all.shape[2]
    tm = min(ROW_TILE, seq)
    tn = COL_TILE
    hpb = tn // HEAD_DIM
    spb = seq // tm
    n_cmp = 2 * N_KV
    assert hpb == n_cmp and (N_HEADS * HEAD_DIM) % tn == 0
    cmp_step = N_HEADS * HEAD_DIM // tn
    return pl.pallas_call(
        functools.partial(_nsa_proj_kernel, cmp_step=cmp_step),
        grid=(m // tm, n_main // tn),
        in_specs=[
            pl.BlockSpec((tm, k), lambda i, j: (i, 0)),
            pl.BlockSpec((None, k, tn), lambda i, j: (layer, 0, j)),
            pl.BlockSpec((k, LANES), lambda i, j: (0, 0)),
        ],
        out_specs=[
            pl.BlockSpec((None, hpb, tm, HEAD_DIM), lambda i, j: (i // spb, j, i % spb, 0)),
            pl.BlockSpec((tm, LANES), lambda i, j: (i, 0)),
            pl.BlockSpec(
                (None, n_cmp, tm // CMP_STRIDE, CMP_STRIDE * HEAD_DIM), lambda i, j: (i // spb, 0, i % spb, 0)
            ),
        ],
        out_shape=[
            jax.ShapeDtypeStruct((bsz, n_main // HEAD_DIM, seq, HEAD_DIM), BF16),
            jax.ShapeDtypeStruct((m, LANES), F32),
            jax.ShapeDtypeStruct((bsz, n_cmp, seq // CMP_STRIDE, CMP_STRIDE * HEAD_DIM), BF16),
        ],
        scratch_shapes=[pltpu.VMEM((tm, HEAD_DIM), F32)],
        compiler_params=_params(
            2, tm * k * 2, k * tn * 2, k * LANES * 2, tm * tn * 2, tm * LANES * 4, tm * tn * 2, tm * tn * 4
        ),
        name="nsa_proj",
    )(xn, w_all, w_gate)


def _matmul_residual_kernel(y_ref, w_ref, h_ref, gn_ref, o_ref, on_ref):
    h = h_ref[...] + _dot(y_ref[...], w_ref[...])
    o_ref[...] = h
    on_ref[...] = _rms_bf16(h, gn_ref[...])


def _matmul_residual(y, w_all, layer, h, g_next):
    m, k = y.shape
    n = w_all.shape[2]
    tm = min(WIDE_ROW_TILE, m)
    return pl.pallas_call(
        _matmul_residual_kernel,
        grid=(m // tm,),
        in_specs=[
            pl.BlockSpec((tm, k), lambda i: (i, 0)),
            pl.BlockSpec((None, k, n), lambda i: (layer, 0, 0)),
            pl.BlockSpec((tm, n), lambda i: (i, 0)),
            pl.BlockSpec((1, n), lambda i: (0, 0)),
        ],
        out_specs=[pl.BlockSpec((tm, n), lambda i: (i, 0)), pl.BlockSpec((tm, n), lambda i: (i, 0))],
        out_shape=[jax.ShapeDtypeStruct((m, n), F32), jax.ShapeDtypeStruct((m, n), BF16)],
        compiler_params=_params(1, tm * k * 2, k * n * 2, tm * n * 4, tm * n * 4, tm * n * 2),
        name="matmul_residual",
    )(y, w_all, h, g_next.reshape(1, n))


def _mlp_kernel(x_ref, xn_ref, wu_ref, wd_ref, o_ref, *, chunk):
    @pl.when(pl.program_id(1) == 0)
    def _():
        o_ref[...] = x_ref[...]

    xn = xn_ref[...]
    for c0 in range(0, wu_ref.shape[1], chunk):
        u = _dot(xn, wu_ref[:, c0:c0 + chunk])
        u = jnp.square(jnp.maximum(u, 0.0)).astype(BF16)
        o_ref[...] += _dot(u, wd_ref[c0:c0 + chunk, :])


def _mlp(x, xn, w_up, w_down, layer):
    m, d = x.shape
    f = w_up.shape[2]
    tm = min(MLP_ROW_TILE, m)
    tf = min(MLP_FF_TILE, f)
    chunk = min(MLP_FF_CHUNK, tf)
    return pl.pallas_call(
        functools.partial(_mlp_kernel, chunk=chunk),
        grid=(m // tm, f // tf),
        in_specs=[
            pl.BlockSpec((tm, d), lambda i, j: (i, 0)),
            pl.BlockSpec((tm, d), lambda i, j: (i, 0)),
            pl.BlockSpec((None, d, tf), lambda i, j: (layer, 0, j)),
            pl.BlockSpec((None, tf, d), lambda i, j: (layer, j, 0)),
        ],
        out_specs=pl.BlockSpec((tm, d), lambda i, j: (i, 0)),
        out_shape=jax.ShapeDtypeStruct((m, d), F32),
        compiler_params=_params(2, tm * d * 4, tm * d * 2, d * tf * 2, tf * d * 2, tm * d * 4),
        name="mlp",
    )(x, xn, w_up, w_down)


def _ple_kernel(x_ref, g_ref, wg_ref, p_ref, wp_ref, gn_ref, o_ref, *, mode):
    x = x_ref[...]
    gate = _sigmoid(_dot(_rms_bf16(x, g_ref[...]), wg_ref[...]))
    proj = _dot(p_ref[...].astype(BF16), wp_ref[...])
    h = x + gate * proj
    if mode == "final":
        ms = jnp.mean(h * h, axis=-1, keepdims=True)
        o_ref[0][...] = h * lax.rsqrt(ms + EPS) * gn_ref[...]
    else:
        o_ref[0][...] = h
        if mode == "next":
            o_ref[1][...] = _rms_bf16(h, gn_ref[...])


def _ple(x, g, w_gate, p_all, layer, w_proj, g_next, mode):
    m, d = x.shape
    e = p_all.shape[-1]
    tm = min(PLE_ROW_TILE, m)
    const = lambda shape: pl.BlockSpec(shape, lambda i: (0,) * len(shape))
    row_spec = pl.BlockSpec((tm, d), lambda i: (i, 0))
    out_specs, out_shape = [row_spec], [jax.ShapeDtypeStruct((m, d), F32)]
    if mode == "next":
        out_specs.append(row_spec)
        out_shape.append(jax.ShapeDtypeStruct((m, d), BF16))

    def body(*refs):
        _ple_kernel(*refs[:6], refs[6:], mode=mode)

    return pl.pallas_call(
        body,
        grid=(m // tm,),
        in_specs=[
            row_spec,
            const((1, d)),
            pl.BlockSpec((None, d, d), lambda i: (layer, 0, 0)),
            pl.BlockSpec((None, tm, e), lambda i: (layer, i, 0)),
            pl.BlockSpec((None, e, d), lambda i: (layer, 0, 0)),
            const((1, d)),
        ],
        out_specs=out_specs,
        out_shape=out_shape,
        compiler_params=_params(1, tm * d * 4, d * d * 2, tm * e * 4, e * d * 2, tm * d * 4, tm * d * 8),
        name="ple",
    )(x, g.reshape(1, d), w_gate, p_all, w_proj, g_next.reshape(1, d))


def _mixer_kernel(
    z_ref, cw_ref, cb_ref, wr_ref, br_ref, wi_ref, bi_ref, lam_ref, pw_ref, ps_ref,
    y_ref, xext_ref, uext_ref, hlast_ref, *, ts, width,
):
    s = pl.program_id(1)
    xh = 8
    uh = 16

    @pl.when(s == 0)
    def _():
        xext_ref[0:xh, :] = jnp.zeros((xh, width), F32)
        uext_ref[0:uh, :] = jnp.zeros((uh, width), F32)
        hlast_ref[...] = jnp.zeros_like(hlast_ref)

    xr = z_ref[:, 0:width]
    gate = z_ref[:, width:2 * width]
    u = z_ref[:, 2 * width:3 * width]

    xext_ref[xh:xh + ts, :] = xr
    conv = cb_ref[...] + cw_ref[CONV_WIDTH - 1:CONV_WIDTH, :] * xr
    for k in range(CONV_WIDTH - 1):
        shift = CONV_WIDTH - 1 - k
        conv = conv + cw_ref[k:k + 1, :] * xext_ref[pl.ds(xh - shift, ts), :]
    xext_ref[0:xh, :] = xr[ts - xh:ts, :]

    gw = wr_ref.shape[1]
    conv_b = conv.astype(BF16)
    r_parts, i_parts = [], []
    for c in range(width // gw):
        xc = conv_b[:, c * gw:(c + 1) * gw]
        r_parts.append(_dot(xc, wr_ref[c]))
        i_parts.append(_dot(xc, wi_ref[c]))
    r = _sigmoid(jnp.concatenate(r_parts, axis=1) + br_ref[...])
    ig = _sigmoid(jnp.concatenate(i_parts, axis=1) + bi_ref[...])

    nl = -lam_ref[...]
    softplus = jnp.maximum(nl, 0.0) + jnp.log(1.0 + jnp.exp(-jnp.abs(nl)))
    log_a = -LRU_C * r * softplus
    a = jnp.exp(log_a)
    b = jnp.sqrt(1.0 - a * a) * (ig * conv)

    sub = 8
    srow = lax.broadcasted_iota(jnp.int32, (sub, 1), 0)
    carry = hlast_ref[...]
    h_groups = []
    for r0 in range(0, ts, sub):
        ag = a[r0:r0 + sub, :]
        bg = b[r0:r0 + sub, :]
        d = 1
        while d < sub:
            keep = srow >= d
            a_sh = jnp.where(keep, pltpu.roll(ag, d, axis=0), 1.0)
            b_sh = jnp.where(keep, pltpu.roll(bg, d, axis=0), 0.0)
            bg = ag * b_sh + bg
            ag = ag * a_sh
            d *= 2
        hg = ag * carry + bg
        carry = hg[sub - 1:sub, :]
        h_groups.append(hg)
    h = jnp.concatenate(h_groups, axis=0)
    hlast_ref[...] = carry
    y_ref[:, 0:width] = (h * _gelu_tanh(gate)).astype(BF16)

    uext_ref[uh:uh + ts, :] = u
    t = s * ts + lax.broadcasted_iota(jnp.int32, (ts, 1), 0)
    gd = width // len(POOL_WINDOWS)
    for gi, win in enumerate(POOL_WINDOWS):
        cs = slice(gi * gd, (gi + 1) * gd)
        wsum = u[:, cs]
        for k in range(1, win):
            wsum = wsum + uext_ref[pl.ds(uh - k, ts), cs]
        cnt = jnp.minimum(t + 1, win).astype(F32)
        dd = wsum / cnt - u[:, cs]
        yp = _dot(dd.astype(BF16), pw_ref[gi]) * ps_ref[:, cs]
        y_ref[:, width + gi * gd:width + (gi + 1) * gd] = yp.astype(BF16)
    uext_ref[0:uh, :] = u[ts - uh:ts, :]


def _mixer(z, conv_w, conv_b, w_r, b_r, w_i, b_i, lam, pool_w, pool_scale):
    bsz, seq, w3 = z.shape
    width = w3 // 3
    ts = min(MIX_TIME_TILE, seq)
    hd = width // LRU_HEADS
    per = 2 * LANES // hd
    gw = per * hd

    def block_diag(w):
        w = w.reshape(LRU_HEADS // per, per, hd, hd)
        eye = jnp.eye(per, dtype=w.dtype)
        return jnp.einsum("cpij,pq->cpiqj", w, eye).reshape(LRU_HEADS // per, gw, gw).astype(BF16)

    row = lambda v: v.reshape(1, width)
    const2 = lambda shape: pl.BlockSpec(shape, lambda b, s: (0,) * len(shape))
    kern = functools.partial(_mixer_kernel, ts=ts, width=width)
    gd = width // len(POOL_WINDOWS)
    return pl.pallas_call(
        kern,
        grid=(bsz, seq // ts),
        in_specs=[
            pl.BlockSpec((None, ts, w3), lambda b, s: (b, s, 0)),
            const2((CONV_WIDTH, width)),
            const2((1, width)),
            const2((LRU_HEADS // per, gw, gw)),
            const2((1, width)),
            const2((LRU_HEADS // per, gw, gw)),
            const2((1, width)),
            const2((1, width)),
            const2((len(POOL_WINDOWS), gd, gd)),
            const2((1, width)),
        ],
        out_specs=pl.BlockSpec((None, ts, 2 * width), lambda b, s: (b, s, 0)),
        out_shape=jax.ShapeDtypeStruct((bsz, seq, 2 * width), BF16),
        scratch_shapes=[
            pltpu.VMEM((ts + 8, width), F32),
            pltpu.VMEM((ts + 16, width), F32),
            pltpu.VMEM((1, width), F32),
        ],
        compiler_params=_params(2, ts * w3 * 4, ts * width * 4, ts * width * 16),
        name="lru_pool_mixer",
    )(
        z, conv_w, row(conv_b), block_diag(w_r), row(b_r), block_diag(w_i), row(b_i), row(lam),
        pool_w.astype(BF16), row(pool_scale),
    )


def _compress_kernel(x_ref, pos_ref, w1_ref, w2_ref, o_ref):
    x = x_ref[...].astype(F32)
    xa = (x + pos_ref[0]).astype(BF16)
    xb = (x + pos_ref[1]).astype(BF16)
    first = _dot(xa, w1_ref[0])
    second = _dot(xb, w1_ref[1])
    n = x.shape[0]
    hid = first + pltpu.roll(second, n - 1, axis=0)
    o_ref[...] = _dot(_gelu_tanh(hid).astype(BF16), w2_ref[...]).astype(o_ref.dtype)


def _compress(zc, pos, w1, w2):
    bsz, _, nc, kk = zc.shape
    two, ng = 2, N_KV
    hid = w1.shape[-1]
    dh = w2.shape[-1]
    return pl.pallas_call(
        _compress_kernel,
        grid=(two, bsz, ng),
        in_specs=[
            pl.BlockSpec((None, None, nc, kk), lambda t, b, g: (b, t * N_KV + g, 0, 0)),
            pl.BlockSpec((None, 2, 1, kk), lambda t, b, g: (t, 0, 0, 0)),
            pl.BlockSpec((None, 2, kk, hid), lambda t, b, g: (t, 0, 0, 0)),
            pl.BlockSpec((None, hid, dh), lambda t, b, g: (t, 0, 0)),
        ],
        out_specs=pl.BlockSpec((None, None, None, nc, dh), lambda t, b, g: (t, b, g, 0, 0)),
        out_shape=jax.ShapeDtypeStruct((two, bsz, ng, nc, dh), BF16),
        compiler_params=_params(3, nc * kk * 2, 2 * kk * hid * 2, hid * dh * 2, nc * dh * 2, nc * kk * 8),
        name="nsa_compress",
    )(zc, pos, w1, w2)


LOG2E = np.float32(1.4426950408889634)
AUG_MASK_ROWS = 64
ALIBI_PARTS = 3


def _t32(x):
    r, c = x.shape
    rows = []
    for cb in range(0, c, LANES):
        rows.append(jnp.concatenate([x[rb:rb + LANES, cb:cb + LANES].T for rb in range(0, r, LANES)], axis=1))
    return jnp.concatenate(rows, axis=0)


def _t_bf16(x):
    return _t32(x.astype(F32)).astype(BF16)


def _nsa_kernel(
    slope_ref, q_ref, kc_ref, vc_ref, ks_ref, vs_ref, kw_ref, vw_ref, gt_ref, ovt_ref, aug_ref, augc_ref,
    o_ref, kp_ref, kwp_ref, kcp_ref, vst_ref, vwt_ref, vct_ref, s_ref, gsc_ref, alibi_ref, *, tq, tk, seq,
):
    g = pl.program_id(1)
    i = pl.program_id(2)
    t0 = i * tq
    nl = HPG * tq
    n_sel = seq // SEL_LEN
    ncp = kc_ref.shape[0]

    lane = lax.broadcasted_iota(jnp.int32, (1, nl), 1)

    @pl.when(i == 0)
    def _():
        kp_ref[:, 0:HEAD_DIM] = ks_ref[...]
        kp_ref[:, HEAD_DIM:2 * HEAD_DIM] = aug_ref[...]
        kwp_ref[:, 0:HEAD_DIM] = kw_ref[...]
        kwp_ref[:, HEAD_DIM:2 * HEAD_DIM] = aug_ref[...]
        kcp_ref[:, 0:HEAD_DIM] = kc_ref[...]
        kcp_ref[:, HEAD_DIM:2 * HEAD_DIM] = augc_ref[...]
        for kt in range(seq // tk):
            vst_ref[kt] = _t_bf16(vs_ref[kt * tk:(kt + 1) * tk, :])
        for kb in range(seq // tq):
            vwt_ref[kb] = _t_bf16(vw_ref[kb * tq:(kb + 1) * tq, :])
        vct_ref[...] = _t_bf16(vc_ref[...])
        slope = jnp.zeros((1, nl), F32)
        for h in range(HPG):
            in_h = (lane >= h * tq) & (lane < (h + 1) * tq)
            slope = jnp.where(in_h, slope_ref[g * HPG + h] * LOG2E, slope)
        arow = lax.broadcasted_iota(jnp.int32, (LANES - AUG_MASK_ROWS, nl), 0)
        rows = jnp.zeros((LANES - AUG_MASK_ROWS, nl), F32)
        rem = slope
        for k in range(ALIBI_PARTS):
            piece = rem.astype(BF16).astype(F32)
            rem = rem - piece
            rows = jnp.where(arow == 2 * k, piece * float(SEL_LEN), rows)
            rows = jnp.where(arow == 2 * k + 1, piece, rows)
        alibi_ref[...] = rows.astype(BF16)

    qscale = np.float32(HEAD_DIM ** -0.5) * LOG2E
    q_t = jnp.concatenate(
        [_t32(q_ref[h].astype(F32) * qscale) for h in range(HPG)], axis=1
    ).astype(BF16)
    t_row = t0 + (lane & (tq - 1))
    alibi_rows = alibi_ref[...]
    qa_plain = jnp.concatenate([q_t, jnp.zeros((AUG_MASK_ROWS, nl), BF16), alibi_rows], axis=0)

    cend = lax.broadcasted_iota(jnp.int32, (ncp, 1), 0) * CMP_STRIDE + (CMP_LEN - 1)
    s_c = jnp.where(cend <= t_row, _dot(kcp_ref[...], qa_plain), NEG)
    m_c = jnp.max(s_c, axis=0, keepdims=True)
    p_c = jnp.exp2(s_c - m_c)
    l_c = jnp.sum(p_c, axis=0, keepdims=True)
    p_c = p_c * jnp.where(m_c > 0.5 * NEG, 1.0 / l_c, 0.0)
    o_c = _dot(vct_ref[...], p_c.astype(BF16))

    p_sum = p_c[:, 0:tq]
    for h in range(1, HPG):
        p_sum = p_sum + p_c[:, h * tq:(h + 1) * tq]
    ovt = ovt_ref[...]
    top_n = min(SEL_TOPN, n_sel)
    sub = 8
    mask_cols = []
    for c0 in range(0, tq, LANES):
        imp = jnp.zeros((n_sel, LANES), F32)
        rem = p_sum[:, c0:c0 + LANES]
        for _ in range(3):
            piece = rem.astype(BF16)
            rem = rem - piece.astype(F32)
            imp = imp + _dot(ovt, piece)
        blk = lax.broadcasted_iota(jnp.int32, (n_sel, LANES), 0)
        tl = t0 + c0 + lax.broadcasted_iota(jnp.int32, (n_sel, LANES), 1)
        cur = lax.shift_right_arithmetic(tl, int(np.log2(SEL_LEN)))
        forced = (blk == 0) | (blk == cur) | (blk == cur - 1)
        causal_s = blk * SEL_LEN <= tl
        v = jnp.where(forced, imp + FORCE_BONUS, imp)
        v = jnp.where(causal_s, v, NEG)
        rank_slabs = []
        for r0 in range(0, n_sel, sub):
            vs_ = v[r0:r0 + sub, :]
            lower = blk[r0:r0 + sub, :]
            rk = jnp.zeros((sub, LANES), F32)
            for j in range(n_sel):
                vj = v[j:j + 1, :]
                if j < r0:
                    beats = vj >= vs_
                elif j >= r0 + sub - 1:
                    beats = vj > vs_
                else:
                    beats = (vj > vs_) | ((vj == vs_) & (lower > j))
                rk = rk + jnp.where(beats, 1.0, 0.0)
            rank_slabs.append(rk)
        rank = jnp.concatenate(rank_slabs, axis=0)
        mask_cols.append(jnp.where((rank < top_n) & causal_s, 0.0, NEG))
    mask_rows = mask_cols[0] if len(mask_cols) == 1 else jnp.concatenate(mask_cols, axis=1)
    if n_sel < AUG_MASK_ROWS:
        mask_rows = jnp.concatenate([mask_rows, jnp.zeros((AUG_MASK_ROWS - n_sel, tq), F32)], axis=0)
    mask_rows = jnp.concatenate([mask_rows] * HPG, axis=1).astype(BF16)
    qa_sel = jnp.concatenate([q_t, mask_rows, alibi_rows], axis=0)

    span = WINDOW + tq
    start = pl.multiple_of(jnp.maximum(t0 - WINDOW, 0), tq)
    dist = t_row - (start + lax.broadcasted_iota(jnp.int32, (span, 1), 0))
    valid_w = lax.bitcast_convert_type(dist, jnp.uint32) < jnp.uint32(WINDOW)
    s_w = jnp.where(valid_w, _dot(kwp_ref[pl.ds(start, span), :], qa_plain), NEG)
    m_w = jnp.max(s_w, axis=0, keepdims=True)
    p_w = jnp.exp2(s_w - m_w)
    l_w = jnp.sum(p_w, axis=0, keepdims=True)
    p_wb = p_w.astype(BF16)
    sb = start // tq
    acc_w = jnp.zeros((HEAD_DIM, nl), F32)
    for k in range(span // tq):
        acc_w = acc_w + _dot(vwt_ref[sb + k], p_wb[k * tq:(k + 1) * tq, :])

    krow = lax.broadcasted_iota(jnp.int32, (tk, 1), 0)

    def score_tile(kt, masked):
        j0 = pl.multiple_of(kt * tk, tk)
        s = _dot(kp_ref[pl.ds(j0, tk), :], qa_sel)
        if masked:
            s = jnp.where(j0 + krow <= t_row, s, NEG)
        s_ref[...] = s
        return jnp.max(s, axis=0, keepdims=True)

    def finish_tile(kt, m_p, l_p, acc, m_t):
        m_n = jnp.maximum(m_p, m_t)
        p = jnp.exp2(s_ref[...] - m_n)
        alpha = jnp.exp2(m_p - m_n)
        l_n = alpha * l_p + jnp.sum(p, axis=0, keepdims=True)
        acc = alpha * acc + _dot(vst_ref[kt], p.astype(BF16))
        return m_n, l_n, acc

    n_kt = (t0 + tq - 1) // tk + 1

    def pipe_step(kt, carry):
        m_p, l_p, acc, m_t, prev = carry
        m_n, l_n, acc = finish_tile(prev, m_p, l_p, acc, m_t)
        return m_n, l_n, acc, score_tile(kt, False), kt

    carry = (
        jnp.full((1, nl), NEG, F32), jnp.zeros((1, nl), F32), jnp.zeros((HEAD_DIM, nl), F32),
        score_tile(n_kt - 1, True), n_kt - 1,
    )
    m_p, l_p, acc, m_t, prev = lax.fori_loop(0, n_kt - 1, pipe_step, carry)
    _, l_s, acc_s = finish_tile(prev, m_p, l_p, acc, m_t)

    gsc_ref[...] = _t32(_sigmoid(gt_ref[...]))

    def gate_row(br):
        first = br * N_HEADS + g * HPG
        return jnp.concatenate([gsc_ref[pl.ds(first + h, 1), :] for h in range(HPG)], axis=1)

    o_t = (
        gate_row(0) * o_c
        + (gate_row(1) * (1.0 / l_s)) * acc_s
        + (gate_row(2) * (1.0 / l_w)) * acc_w
    )
    for h in range(HPG):
        o_ref[:, h * HEAD_DIM:(h + 1) * HEAD_DIM] = _t32(o_t[:, h * tq:(h + 1) * tq]).astype(o_ref.dtype)


def _nsa_attention(z, kv_cmp, gates):
    bsz, _, seq, _ = z.shape
    tq = min(ATT_Q_TILE, seq)
    tk = min(ATT_K_TILE, seq)
    n_sel = seq // SEL_LEN
    ncp = kv_cmp.shape[3]
    qw = HPG * HEAD_DIM
    kv0 = N_HEADS
    assert n_sel <= AUG_MASK_ROWS and tq & (tq - 1) == 0 and seq % tk == 0 and tk % tq == 0

    cs = np.arange(ncp)[None, :] * CMP_STRIDE
    ss = np.arange(n_sel)[:, None] * SEL_LEN
    ovt = ((cs < ss + SEL_LEN) & (cs + CMP_LEN > ss) & (np.arange(ncp)[None, :] < ncp - 1)).astype(np.float32)

    def aug_cols(pos):
        a = np.zeros((pos.shape[0], LANES), np.float32)
        a[np.arange(pos.shape[0]), np.minimum(pos // SEL_LEN, AUG_MASK_ROWS - 1)] = 1.0
        for k in range(ALIBI_PARTS):
            a[:, AUG_MASK_ROWS + 2 * k] = pos // SEL_LEN
            a[:, AUG_MASK_ROWS + 2 * k + 1] = pos % SEL_LEN
        return jnp.asarray(a, BF16)

    aug = aug_cols(np.arange(seq))
    augc = aug_cols(np.arange(ncp) * CMP_STRIDE + CMP_LEN - 1)
    slopes = jnp.exp2(-8.0 * jnp.arange(1, N_HEADS + 1, dtype=F32) / N_HEADS)

    def kv_spec(col):
        return pl.BlockSpec((None, None, seq, HEAD_DIM), lambda b, g, i, col=col: (b, kv0 + col * N_KV + g, 0, 0))

    const = lambda shape: pl.BlockSpec(shape, lambda b, g, i: (0,) * len(shape))
    kern = functools.partial(_nsa_kernel, tq=tq, tk=tk, seq=seq)
    return pl.pallas_call(
        kern,
        grid=(bsz, N_KV, seq // tq),
        in_specs=[
            pl.BlockSpec(memory_space=pltpu.SMEM),
            pl.BlockSpec((None, HPG, tq, HEAD_DIM), lambda b, g, i: (b, g, i, 0)),
            pl.BlockSpec((None, None, None, ncp, HEAD_DIM), lambda b, g, i: (0, b, g, 0, 0)),
            pl.BlockSpec((None, None, None, ncp, HEAD_DIM), lambda b, g, i: (1, b, g, 0, 0)),
            kv_spec(2), kv_spec(3), kv_spec(4), kv_spec(5),
            pl.BlockSpec((None, tq, LANES), lambda b, g, i: (b, i, 0)),
            const((n_sel, ncp)),
            const((seq, LANES)),
            const((ncp, LANES)),
        ],
        out_specs=pl.BlockSpec((None, tq, qw), lambda b, g, i: (b, i, g)),
        out_shape=jax.ShapeDtypeStruct((bsz, seq, N_HEADS * HEAD_DIM), BF16),
        scratch_shapes=[
            pltpu.VMEM((seq, 2 * HEAD_DIM), BF16),
            pltpu.VMEM((seq, 2 * HEAD_DIM), BF16),
            pltpu.VMEM((ncp, 2 * HEAD_DIM), BF16),
            pltpu.VMEM((seq // tk, HEAD_DIM, tk), BF16),
            pltpu.VMEM((seq // tq, HEAD_DIM, tq), BF16),
            pltpu.VMEM((HEAD_DIM, ncp), BF16),
            pltpu.VMEM((tk, HPG * tq), F32),
            pltpu.VMEM((LANES, tq), F32),
            pltpu.VMEM((LANES - AUG_MASK_ROWS, HPG * tq), BF16),
        ],
        compiler_params=_params(3, 5 * seq * HEAD_DIM * 2, 3 * seq * HEAD_DIM * 2, HPG * tq * tk * 12),
        name="nsa_attention",
    )(slopes, z, kv_cmp, kv_cmp, z, z, z, z, gates, jnp.asarray(ovt, BF16), aug, augc)


def _ab_layer(h, bsz, seq, j, ln_g, w_in, conv_w, conv_b, w_r, b_r, w_i, b_i, lam, pool_w, pool_scale, w_out, g_next):
    z = _norm_matmul(h, ln_g, w_in, j, F32)
    y = _mixer(z.reshape(bsz, seq, -1), conv_w, conv_b, w_r, b_r, w_i, b_i, lam, pool_w, pool_scale)
    return _matmul_residual(y.reshape(bsz * seq, -1), w_out, j, h, g_next)


def _nsa_layer(h, hn, bsz, seq, j, w_in, w_gate, pos_k, w1_k, w2_k, pos_v, w1_v, w2_v, w_out, g_next):
    z, gl, zc = _nsa_proj(hn, w_in, j, w_gate, bsz, seq)
    half = CMP_LEN // CMP_STRIDE
    pos = jnp.stack([pos_k, pos_v]).reshape(2, half, 1, CMP_STRIDE * HEAD_DIM)
    w1 = jnp.stack([w1_k, w1_v]).reshape(2, half, CMP_STRIDE * HEAD_DIM, -1).astype(BF16)
    w2 = jnp.stack([w2_k, w2_v]).astype(BF16)
    kv_cmp = _compress(zc, pos, w1, w2)

    o = _nsa_attention(z, kv_cmp, gl.reshape(bsz, seq, LANES))
    return _matmul_residual(o.reshape(bsz * seq, -1), w_out, j, h, g_next)


def kernel(x, p, ln_mix_g, ab_w_in, ab_conv_w, ab_conv_b, ab_w_rgate, ab_b_rgate, ab_w_igate, ab_b_igate, ab_lambda, ab_pool_w, ab_pool_scale, ab_w_out, c_w_in, c_cmp_pos_k, c_cmp_w1_k, c_cmp_w2_k, c_cmp_pos_v, c_cmp_w1_v, c_cmp_w2_v, c_w_out, ln_mlp_g, mlp_w_up, mlp_w_down, ln_ple_g, ple_w_gate, ple_w_proj, ln_final_g):
    bsz, seq, d = x.shape
    depth = p.shape[0]
    h = x.reshape(bsz * seq, d)
    p_all = p.reshape(depth, bsz * seq, -1)
    n_main = N_HEADS * HEAD_DIM + 6 * N_KV * HEAD_DIM
    n_gate = N_BRANCH * N_HEADS
    ab_in, ab_out = ab_w_in.astype(BF16), ab_w_out.astype(BF16)
    c_in, c_out = c_w_in[:, :, :n_main].astype(BF16), c_w_out.astype(BF16)
    w_up, w_down = mlp_w_up.astype(BF16), mlp_w_down.astype(BF16)
    w_pg, w_pp = ple_w_gate.astype(BF16), ple_w_proj.astype(BF16)
    hn = None
    for i in range(depth):
        j = i // 2
        if i % 2 == 0:
            h, hn = _ab_layer(
                h, bsz, seq, j, ln_mix_g[i], ab_in, ab_conv_w[j], ab_conv_b[j], ab_w_rgate[j], ab_b_rgate[j],
                ab_w_igate[j], ab_b_igate[j], ab_lambda[j], ab_pool_w[j], ab_pool_scale[j], ab_out, ln_mlp_g[i],
            )
        else:
            w_gate = jnp.pad(c_w_in[j, :, n_main:], ((0, 0), (0, LANES - n_gate))).astype(BF16)
            h, hn = _nsa_layer(
                h, hn, bsz, seq, j, c_in, w_gate, c_cmp_pos_k[j], c_cmp_w1_k[j], c_cmp_w2_k[j],
                c_cmp_pos_v[j], c_cmp_w1_v[j], c_cmp_w2_v[j], c_out, ln_mlp_g[i],
            )
        h = _mlp(h, hn, w_up, w_down, i)
        if i == depth - 1:
            (h,) = _ple(h, ln_ple_g[i], w_pg, p_all, i, w_pp, ln_final_g, "final")
        elif (i + 1) % 2 == 1:
            h, hn = _ple(h, ln_ple_g[i], w_pg, p_all, i, w_pp, ln_mix_g[i + 1], "next")
        else:
            (h,) = _ple(h, ln_ple_g[i], w_pg, p_all, i, w_pp, ln_final_g, "plain")
    return h.reshape(bsz, seq, d)
```
